```python
import math
import jax, jax.numpy as jnp
from jax import lax
import numpy as np

D_MODEL = 2048
BATCH = 4
SEQ = 4096
DEPTH = 2

HEAD_DIM = 128
H_A = 4
H_B = 6
H_C = 6
N_HEADS = H_A + H_B + H_C
H_SOFT = H_B + H_C
MIX_WIDTH = N_HEADS * HEAD_DIM
N_BRANCH = 3
D_FF = -(-(8 * D_MODEL) // (3 * 256)) * 256
Q_BLOCK = 128
MOBA_BLOCK = 256
MOBA_TOPK = 3
MOBA_Q_CHUNK = 32
DILATIONS = ((128, 1), (512, 4), (2048, 16))
BAND_BLOCK = 128
N_BUCKETS = 32
MAX_DISTANCE = 2048
RMS_EPS = 1e-6
NEG_INF = -1e30

kernel_name = 'hybrid_sb_moba_dilated_block'


def rmsnorm(x, g):
    xf = x.astype(jnp.float32)
    y = xf * lax.rsqrt(jnp.mean(xf * xf, axis=-1, keepdims=True) + RMS_EPS)
    return (y * g.astype(jnp.float32)).astype(x.dtype)


def t5_bucket(dist):
    max_exact = N_BUCKETS // 2
    d = jnp.maximum(dist, 0)
    df = jnp.maximum(d, 1).astype(jnp.float32)
    large = max_exact + (jnp.log(df / max_exact) / math.log(MAX_DISTANCE / max_exact)
                         * (N_BUCKETS - max_exact)).astype(jnp.int32)
    large = jnp.minimum(large, N_BUCKETS - 1)
    return jnp.where(d < max_exact, d, large)


def stick_breaking_attention(q, k, v):
    B, H, S, hd = q.shape
    scale = hd ** -0.5
    kpos = jnp.arange(S)

    def block(i):
        t0 = i * Q_BLOCK
        qb = lax.dynamic_slice_in_dim(q, t0, Q_BLOCK, axis=2)
        z = jnp.einsum('bhqd,bhkd->bhqk', qb, k).astype(jnp.float32) * scale
        qpos = t0 + jnp.arange(Q_BLOCK)
        past = kpos[None, :] < qpos[:, None]
        log_beta = jax.nn.log_sigmoid(z)
        log_1mb = jnp.where(past, jax.nn.log_sigmoid(-z), 0.0)
        after = lax.cumsum(log_1mb, axis=3, reverse=True) - log_1mb
        w = jnp.where(past, jnp.exp(log_beta + after), 0.0)
        return jnp.einsum('bhqk,bhkd->bhqd', w.astype(v.dtype), v)

    out = lax.map(block, jnp.arange(S // Q_BLOCK))
    return jnp.moveaxis(out, 0, 2).reshape(B, H, S, hd)


def moba_attention(q, k, v, bias_table):
    B, H, S, hd = q.shape
    nb = -(-S // MOBA_BLOCK)
    s_pad = nb * MOBA_BLOCK
    pad = ((0, 0), (0, 0), (0, s_pad - S), (0, 0))
    kp, vp = jnp.pad(k, pad), jnp.pad(v, pad)
    kblk = kp.reshape(B, H, nb, MOBA_BLOCK, hd)
    vblk = vp.reshape(B, H, nb, MOBA_BLOCK, hd)
    scale = hd ** -0.5
    bias_hb = bias_table.T.astype(jnp.float32)
    n_sel = min(MOBA_TOPK, nb - 1)
    own_blk = jnp.arange(S) // MOBA_BLOCK
    if n_sel > 0:
        kmean = jnp.mean(kblk, axis=3)
        gate = jnp.einsum('bhsd,bhnd->bhsn', q, kmean).astype(jnp.float32)
        fully_past = jnp.arange(nb)[None, :] < own_blk[:, None]
        gate = jnp.where(fully_past, gate, NEG_INF)
        _, sel = lax.top_k(gate, n_sel)
        sel_ok = sel < own_blk[:, None]
    b_idx = jnp.arange(B)[:, None, None, None]
    h_idx = jnp.arange(H)[None, :, None, None]
    in_blk = jnp.arange(MOBA_BLOCK)

    def chunk(i):
        t0 = i * MOBA_Q_CHUNK
        qc = lax.dynamic_slice_in_dim(q, t0, MOBA_Q_CHUNK, axis=2)
        tq = t0 + jnp.arange(MOBA_Q_CHUNK)
        b0 = (t0 // MOBA_BLOCK) * MOBA_BLOCK
        k_own = lax.dynamic_slice_in_dim(kp, b0, MOBA_BLOCK, axis=2)
        v_own = lax.dynamic_slice_in_dim(vp, b0, MOBA_BLOCK, axis=2)
        dist = tq[:, None] - (b0 + in_blk)[None, :]
        l_own = jnp.einsum('bhqd,bhkd->bhqk', qc, k_own).astype(jnp.float32) * scale
        l_own = jnp.where(dist >= 0, l_own + bias_hb[:, t5_bucket(dist)], NEG_INF)
        if n_sel == 0:
            p = jax.nn.softmax(l_own, axis=-1).astype(v.dtype)
            return jnp.einsum('bhqk,bhkd->bhqd', p, v_own)
        sc = lax.dynamic_slice_in_dim(sel, t0, MOBA_Q_CHUNK, axis=2)
        ok = lax.dynamic_slice_in_dim(sel_ok, t0, MOBA_Q_CHUNK, axis=2)
        k_sel = kblk[b_idx, h_idx, sc]
        v_sel = vblk[b_idx, h_idx, sc]
        dist_sel = tq[:, None, None] - (sc[..., None] * MOBA_BLOCK + in_blk)
        l_sel = jnp.einsum('bhqd,bhqnkd->bhqnk', qc, k_sel).astype(jnp.float32) * scale
        l_sel = l_sel + bias_hb[h_idx[..., None], t5_bucket(dist_sel)]
        n_past = n_sel * MOBA_BLOCK
        l_sel = jnp.where(ok[..., None], l_sel, NEG_INF).reshape(B, H, MOBA_Q_CHUNK, n_past)
        p = jax.nn.softmax(jnp.concatenate([l_sel, l_own], axis=-1), axis=-1).astype(v.dtype)
        o = jnp.einsum('bhqk,bhqkd->bhqd', p[..., :n_past],
                       v_sel.reshape(B, H, MOBA_Q_CHUNK, n_past, hd))
        return o + jnp.einsum('bhqk,bhkd->bhqd', p[..., n_past:], v_own)

    out = lax.map(chunk, jnp.arange(S // MOBA_Q_CHUNK))
    return jnp.moveaxis(out, 0, 2).reshape(B, H, S, hd)


def dilated_attention(q, k, v, bias_table):
    B, H, S, hd = q.shape
    scale = hd ** -0.5
    bb = BAND_BLOCK
    qi = jnp.arange(bb)[:, None]
    kj = jnp.arange(2 * bb)[None, :]
    delta = qi + bb - kj
    outs, lses = [], []
    for window, r in DILATIONS:
        span = window // r
        L = S // r
        nb = -(-L // bb)
        Lp = nb * bb

        def to_sub(t):
            t = t.reshape(B, H, L, r, hd).swapaxes(2, 3)
            return jnp.pad(t, ((0, 0), (0, 0), (0, 0), (0, Lp - L), (0, 0)))

        def band(t):
            t = jnp.pad(to_sub(t), ((0, 0), (0, 0), (0, 0), (bb, 0), (0, 0))).reshape(B, H, r, nb + 1, bb, hd)
            return jnp.concatenate([t[:, :, :, :-1], t[:, :, :, 1:]], axis=4)

        qs = to_sub(q).reshape(B, H, r, nb, bb, hd)
        kb, vb = band(k), band(v)
        key_idx = (jnp.arange(nb)[:, None, None] - 1) * bb + kj[None]
        mask = (delta >= 0) & (delta <= span) & (key_idx >= 0)
        bias = jnp.transpose(bias_table[t5_bucket(delta * r)], (2, 0, 1)).astype(jnp.float32)
        logits = jnp.einsum('bhrnqd,bhrnkd->bhrnqk', qs, kb).astype(jnp.float32) * scale
        logits = jnp.where(mask, logits + bias[:, None, None], NEG_INF)
        m = jnp.max(logits, axis=-1, keepdims=True)
        p = jnp.exp(logits - m)
        den = jnp.sum(p, axis=-1, keepdims=True)
        o = jnp.einsum('bhrnqk,bhrnkd->bhrnqd', (p / den).astype(v.dtype), vb)
        lse = (m + jnp.log(den))[..., 0]

        def from_sub(t):
            t = t.reshape((B, H, r, Lp) + t.shape[5:])[:, :, :, :L]
            return jnp.swapaxes(t, 2, 3).reshape((B, H, S) + t.shape[4:])

        outs.append(from_sub(o))
        lses.append(from_sub(lse))
    w = jax.nn.softmax(jnp.stack(lses, axis=0), axis=0)
    return jnp.einsum('gbhs,gbhsd->bhsd', w.astype(v.dtype), jnp.stack(outs, axis=0))


def hybrid_layer(x, g_mix, w_in, q_gain, k_gain, w_branch, w_out, g_ffn, w_gu, w_down, rel_bias):
    B, S, _ = x.shape
    h = rmsnorm(x, g_mix)
    proj = h @ w_in
    qkv = proj[..., :3 * MIX_WIDTH].reshape(B, S, 3, N_HEADS, HEAD_DIM)
    qkv = jnp.transpose(qkv, (2, 0, 3, 1, 4))
    q, k, v = qkv[0], qkv[1], qkv[2]
    gates = jax.nn.sigmoid(proj[..., 3 * MIX_WIDTH:].astype(jnp.float32)).astype(x.dtype)
    gates = gates.reshape(B, S, N_BRANCH, D_MODEL)

    o_a = stick_breaking_attention(q[:, :H_A], k[:, :H_A], v[:, :H_A])
    qn = rmsnorm(q[:, H_A:], q_gain[:, None, :])
    kn = rmsnorm(k[:, H_A:], k_gain[:, None, :])
    o_b = moba_attention(qn[:, :H_B], kn[:, :H_B], v[:, H_A:H_A + H_B], rel_bias[:, :H_B])
    o_c = dilated_attention(qn[:, H_B:], kn[:, H_B:], v[:, H_A + H_B:], rel_bias[:, H_B:])

    def flat(o):
        return jnp.transpose(o, (0, 2, 1, 3)).reshape(B, S, -1)

    ea, eb = H_A * HEAD_DIM, (H_A + H_B) * HEAD_DIM
    merged = (gates[:, :, 0] * (flat(o_a) @ w_branch[:ea])
              + gates[:, :, 1] * (flat(o_b) @ w_branch[ea:eb])
              + gates[:, :, 2] * (flat(o_c) @ w_branch[eb:]))
    x = x + merged @ w_out

    h2 = rmsnorm(x, g_ffn)
    gu = h2 @ w_gu
    x = x + (jax.nn.silu(gu[..., :D_FF]) * gu[..., D_FF:]) @ w_down
    return x


def setup_inputs(seed: int = 0) -> dict:
    key = jax.random.key(seed)
    ks = jax.random.split(key, 11)

    def nrm(k, shape, s):
        return jax.random.normal(k, shape, jnp.float32) * s

    return {
        'x': nrm(ks[0], (BATCH, SEQ, D_MODEL), 1.0),
        'g_mix': 1.0 + nrm(ks[1], (DEPTH, D_MODEL), 0.02),
        'w_in': nrm(ks[2], (DEPTH, D_MODEL, 3 * MIX_WIDTH + N_BRANCH * D_MODEL), D_MODEL ** -0.5),
        'q_gain': 1.0 + nrm(ks[3], (DEPTH, H_SOFT, HEAD_DIM), 0.02),
        'k_gain': 1.0 + nrm(ks[4], (DEPTH, H_SOFT, HEAD_DIM), 0.02),
        'w_branch': nrm(ks[5], (DEPTH, MIX_WIDTH, D_MODEL), (MIX_WIDTH / N_BRANCH) ** -0.5),
        'w_out': nrm(ks[6], (DEPTH, D_MODEL, D_MODEL), D_MODEL ** -0.5),
        'g_ffn': 1.0 + nrm(ks[7], (DEPTH, D_MODEL), 0.02),
        'w_gu': nrm(ks[8], (DEPTH, D_MODEL, 2 * D_FF), D_MODEL ** -0.5),
        'w_down': nrm(ks[9], (DEPTH, D_FF, D_MODEL), D_FF ** -0.5),
        'rel_bias': nrm(ks[10], (N_BUCKETS, H_SOFT), 0.5),
    }


def reference(x, g_mix, w_in, q_gain, k_gain, w_branch, w_out, g_ffn, w_gu, w_down, rel_bias):
    for l in range(DEPTH):
        x = hybrid_layer(x, g_mix[l], w_in[l], q_gain[l], k_gain[l], w_branch[l], w_out[l],
                         g_ffn[l], w_gu[l], w_down[l], rel_bias)
    return x
```

```python
import functools
import math

import jax
import jax.numpy as jnp
from jax import lax
from jax.experimental import pallas as pl
from jax.experimental.pallas import tpu as pltpu

D_MODEL = 2048
HEAD_DIM = 128
H_A = 4
H_B = 6
H_C = 6
N_HEADS = H_A + H_B + H_C
MIX_WIDTH = N_HEADS * HEAD_DIM
N_BRANCH = 3
D_FF = 5632
MOBA_BLOCK = 256
MOBA_TOPK = 3
DILATIONS = ((128, 1), (512, 4), (2048, 16))
BAND_BLOCK = 128
N_BUCKETS = 32
MAX_DISTANCE = 2048
RMS_EPS = 1e-6
NEG_INF = -1e30

LANES = 128
VMEM_LIMIT = 56 * 1024 * 1024

F32 = jnp.float32
BF16 = jnp.bfloat16


def _params(sem):
    return pltpu.CompilerParams(dimension_semantics=sem, vmem_limit_bytes=VMEM_LIMIT)


def _dot(a, b):
    return jnp.dot(a, b, preferred_element_type=F32)


def _dot_nt(a, b):
    return lax.dot_general(a, b, (((1,), (1,)), ((), ())), preferred_element_type=F32)


def _rms_rows(x):
    return x * lax.rsqrt(jnp.mean(x * x, axis=-1, keepdims=True) + RMS_EPS)


def _qkv_kernel(x_ref, g_ref, w_ref, cg_ref, o_ref, h_ref, *, heads_per_tile):
    j = pl.program_id(2)

    @pl.when(j == 0)
    def _():
        h_ref[...] = (_rms_rows(x_ref[0]) * g_ref[...]).astype(BF16)

    acc = _dot(h_ref[...], w_ref[...])
    cg = cg_ref[...]
    tiles_per_part = N_HEADS // heads_per_tile
    normed = (j < 2 * tiles_per_part) & (j % tiles_per_part != 0)

    @pl.when(normed)
    def _():
        for hh in range(heads_per_tile):
            sl = slice(hh * LANES, (hh + 1) * LANES)
            o_ref[0, hh] = (_rms_rows(acc[:, sl]) * cg[:, sl]).astype(BF16)

    @pl.when(jnp.logical_not(normed))
    def _():
        y = acc * cg
        for hh in range(heads_per_tile):
            o_ref[0, hh] = y[:, hh * LANES:(hh + 1) * LANES].astype(BF16)


def _qkv_proj(x, g, w_qkv, colgain, *, tm, tn):
    B, S, D = x.shape
    hpt = tn // LANES
    assert H_A % hpt == 0 and N_HEADS % hpt == 0
    n_tiles = 3 * MIX_WIDTH // tn
    return pl.pallas_call(
        functools.partial(_qkv_kernel, heads_per_tile=hpt),
        grid=(B, S // tm, n_tiles),
        in_specs=[
            pl.BlockSpec((1, tm, D), lambda b, s, j: (b, s, 0)),
            pl.BlockSpec((1, D), lambda b, s, j: (0, 0)),
            pl.BlockSpec((D, tn), lambda b, s, j: (0, j)),
            pl.BlockSpec((1, tn), lambda b, s, j: (0, j)),
        ],
        out_specs=pl.BlockSpec((1, hpt, tm, LANES), lambda b, s, j: (b, j, s, 0)),
        out_shape=jax.ShapeDtypeStruct((B, 3 * N_HEADS, S, LANES), BF16),
        scratch_shapes=[pltpu.VMEM((tm, D), BF16)],
        compiler_params=_params(("arbitrary", "arbitrary", "arbitrary")),
        name="qkv_proj",
    )(x, g, w_qkv, colgain)


def _gates_kernel(x_ref, g_ref, w_ref, o_ref, h_ref):
    @pl.when(pl.program_id(2) == 0)
    def _():
        h_ref[...] = (_rms_rows(x_ref[0]) * g_ref[...]).astype(BF16)

    o_ref[0] = jax.nn.sigmoid(_dot(h_ref[...], w_ref[...])).astype(BF16)


def _gates_proj(x, g, w_gate, *, tm, tn):
    B, S, D = x.shape
    N = w_gate.shape[1]
    return pl.pallas_call(
        _gates_kernel,
        grid=(B, S // tm, N // tn),
        in_specs=[
            pl.BlockSpec((1, tm, D), lambda b, s, j: (b, s, 0)),
            pl.BlockSpec((1, D), lambda b, s, j: (0, 0)),
            pl.BlockSpec((D, tn), lambda b, s, j: (0, j)),
        ],
        out_specs=pl.BlockSpec((1, tm, tn), lambda b, s, j: (b, s, j)),
        out_shape=jax.ShapeDtypeStruct((B, S, N), BF16),
        scratch_shapes=[pltpu.VMEM((tm, D), BF16)],
        compiler_params=_params(("arbitrary", "arbitrary", "arbitrary")),
        name="gates_proj",
    )(x, g, w_gate)


def _softplus(z):
    return jnp.maximum(z, 0.0) + jnp.log(1.0 + jnp.exp(-jnp.abs(z)))


def _sb_kernel(q_ref, k_ref, v_ref, u_ref, o_ref, *, blk):
    i = pl.program_id(2)
    q = q_ref[0, 0]
    u = u_ref[...]
    row = lax.broadcasted_iota(jnp.int32, (blk, blk), 0)
    col = lax.broadcasted_iota(jnp.int32, (blk, blk), 1)
    past = col < row

    def block(j, c, acc, diag):
        start = pl.multiple_of(j * blk, blk)
        k = k_ref[0, 0, pl.ds(start, blk), :]
        v = v_ref[0, 0, pl.ds(start, blk), :]
        z = _dot_nt(q, k)
        sp = _softplus(z)
        spm = jnp.where(past, sp, 0.0) if diag else sp
        hi = spm.astype(BF16)
        lo = (spm - hi.astype(F32)).astype(BF16)
        after = _dot(hi, u) + _dot(lo, u)
        w = jnp.exp(z - sp - after - c)
        if diag:
            w = jnp.where(past, w, 0.0)
        acc = acc + _dot(w.astype(BF16), v)
        c = c + jnp.sum(spm, axis=-1, keepdims=True)
        return c, acc

    c, acc = block(i, jnp.zeros((blk, 1), F32), jnp.zeros((blk, HEAD_DIM), F32), True)
    c, acc = lax.fori_loop(0, i, lambda t, ca: block(i - 1 - t, ca[0], ca[1], False), (c, acc))
    o_ref[0] = acc.astype(BF16)


def _sb_attention(qkv, *, blk):
    B, _, S, _ = qkv.shape
    tri = (jnp.arange(blk)[:, None] > jnp.arange(blk)[None, :]).astype(BF16)
    return pl.pallas_call(
        functools.partial(_sb_kernel, blk=blk),
        grid=(B, H_A, S // blk),
        in_specs=[
            pl.BlockSpec((1, 1, blk, LANES), lambda b, h, i: (b, h, i, 0)),
            pl.BlockSpec((1, 1, S, LANES), lambda b, h, i: (b, N_HEADS + h, 0, 0)),
            pl.BlockSpec((1, 1, S, LANES), lambda b, h, i: (b, 2 * N_HEADS + h, 0, 0)),
            pl.BlockSpec((blk, blk), lambda b, h, i: (0, 0)),
        ],
        out_specs=pl.BlockSpec((1, blk, LANES), lambda b, h, i: (b, i, h)),
        out_shape=jax.ShapeDtypeStruct((B, S, H_A * HEAD_DIM), BF16),
        compiler_params=_params(("arbitrary", "arbitrary", "arbitrary")),
        name="sb_attention",
    )(qkv, qkv, qkv, tri)


def _toeplitz(row_vals, n_rows, shift):
    full = jnp.broadcast_to(row_vals, (n_rows, row_vals.shape[1]))
    return pltpu.roll(full, shift, 1, stride=1, stride_axis=0)


def _moba_kernel(q_ref, k_ref, v_ref, tab_ref, o_ref, tb_ref, km_ref, *, nblk, n_sel):
    b = pl.program_id(1)
    i = pl.program_id(2)
    blk = MOBA_BLOCK

    @pl.when((b == 0) & (i == 0))
    def _():
        for d in range(nblk):
            tb_ref[d] = _toeplitz(tab_ref[0, d:d + 1, :], blk, blk + 1)[:, :blk]

    @pl.when(i == 0)
    def _():
        km_ref[...] = jnp.zeros_like(km_ref)
        for n in range(nblk):
            kb = k_ref[0, 0, n * blk:(n + 1) * blk, :].astype(F32)
            km_ref[n:n + 1, :] = jnp.mean(kb, axis=0, keepdims=True)

    q = q_ref[0, 0]
    lane = lax.broadcasted_iota(jnp.int32, (blk, LANES), 1)
    fully_past = lane < i
    gate = jnp.where(fully_past, _dot_nt(q, km_ref[...].astype(BF16)), NEG_INF)
    sel = jnp.zeros((blk, LANES), F32)
    for _ in range(n_sel):
        top = jnp.max(gate, axis=-1, keepdims=True)
        first = jnp.min(jnp.where(gate == top, lane, LANES), axis=-1, keepdims=True)
        pick = lane == first
        sel = jnp.where(pick, 1.0, sel)
        gate = jnp.where(pick, -jnp.inf, gate)
    sel = jnp.where(fully_past, sel, 0.0)

    row = lax.broadcasted_iota(jnp.int32, (blk, blk), 0)
    col = lax.broadcasted_iota(jnp.int32, (blk, blk), 1)

    def scores(n, d):
        start = pl.multiple_of(n * blk, blk)
        k = k_ref[0, 0, pl.ds(start, blk), :]
        v = v_ref[0, 0, pl.ds(start, blk), :]
        return _dot_nt(q, k) + tb_ref[d], v

    s, v = scores(i, 0)
    s = jnp.where(col <= row, s, NEG_INF)
    m = jnp.max(s, axis=-1, keepdims=True)
    p = jnp.exp(s - m)
    l = jnp.sum(p, axis=-1, keepdims=True)
    acc = _dot(p.astype(BF16), v)

    def body(n, carry):
        m, l, acc = carry
        s, v = scores(n, i - n)
        chosen = jnp.max(jnp.where(lane == n, sel, 0.0), axis=-1, keepdims=True)
        s = jnp.where(chosen > 0.0, s, NEG_INF)
        m_new = jnp.maximum(m, jnp.max(s, axis=-1, keepdims=True))
        alpha = jnp.exp(m - m_new)
        p = jnp.exp(s - m_new)
        l = alpha * l + jnp.sum(p, axis=-1, keepdims=True)
        acc = alpha * acc + _dot(p.astype(BF16), v)
        return m_new, l, acc

    m, l, acc = lax.fori_loop(0, i, body, (m, l, acc))
    o_ref[0] = (acc / l).astype(BF16)


def _moba_attention(qkv, tab):
    B, _, S, _ = qkv.shape
    blk = MOBA_BLOCK
    nblk = S // blk
    n_sel = min(MOBA_TOPK, nblk - 1)
    return pl.pallas_call(
        functools.partial(_moba_kernel, nblk=nblk, n_sel=n_sel),
        grid=(H_B, B, nblk),
        in_specs=[
            pl.BlockSpec((1, 1, blk, LANES), lambda h, b, i: (b, H_A + h, i, 0)),
            pl.BlockSpec((1, 1, S, LANES), lambda h, b, i: (b, N_HEADS + H_A + h, 0, 0)),
            pl.BlockSpec((1, 1, S, LANES), lambda h, b, i: (b, 2 * N_HEADS + H_A + h, 0, 0)),
            pl.BlockSpec((1, nblk, 2 * blk), lambda h, b, i: (h, 0, 0)),
        ],
        out_specs=pl.BlockSpec((1, blk, LANES), lambda h, b, i: (b, i, h)),
        out_shape=jax.ShapeDtypeStruct((B, S, H_B * HEAD_DIM), BF16),
        scratch_shapes=[pltpu.VMEM((nblk, blk, blk), F32), pltpu.VMEM((LANES, LANES), F32)],
        compiler_params=_params(("arbitrary", "arbitrary", "arbitrary")),
        name="moba_attention",
    )(qkv, qkv, qkv, tab)


def _dilated_kernel(q_ref, k_ref, v_ref, tab_ref, o_ref, lse_ref, *, r, nb):
    bb = BAND_BLOCK
    qi = lax.broadcasted_iota(jnp.int32, (bb, 2 * bb), 0)
    kj = lax.broadcasted_iota(jnp.int32, (bb, 2 * bb), 1)
    bias = jnp.where((kj >= qi) & (kj <= qi + bb), _toeplitz(tab_ref[0], bb, 0), NEG_INF)

    for c in range(r):
        cs = slice(c * LANES, (c + 1) * LANES)

        def attend(n, s, v):
            m = jnp.max(s, axis=-1, keepdims=True)
            p = jnp.exp(s - m)
            den = jnp.sum(p, axis=-1, keepdims=True)
            o = _dot(p.astype(BF16), v) / den
            rows = pl.ds(pl.multiple_of(n * bb, bb), bb)
            o_ref[0, 0, rows, cs] = o.astype(BF16)
            lse_ref[0, 0, rows, cs] = jnp.broadcast_to(m + jnp.log(den), (bb, LANES))

        attend(0, _dot_nt(q_ref[0, 0, 0:bb, cs], k_ref[0, 0, 0:bb, cs]) + bias[:, bb:], v_ref[0, 0, 0:bb, cs])

        def body(n, _):
            q = q_ref[0, 0, pl.ds(pl.multiple_of(n * bb, bb), bb), cs]
            keys = pl.ds(pl.multiple_of((n - 1) * bb, bb), 2 * bb)
            attend(n, _dot_nt(q, k_ref[0, 0, keys, cs]) + bias, v_ref[0, 0, keys, cs])
            return 0

        lax.fori_loop(1, nb, body, 0)


def _dilated_pattern(qkv, tab, r):
    B, _, S, _ = qkv.shape
    L = S // r
    nb = L // BAND_BLOCK
    view = qkv.reshape(B, 3 * N_HEADS, L, r * LANES)
    h0 = H_A + H_B
    blk = (1, 1, L, r * LANES)
    o, lse = pl.pallas_call(
        functools.partial(_dilated_kernel, r=r, nb=nb),
        grid=(B, H_C),
        in_specs=[
            pl.BlockSpec(blk, lambda b, h: (b, h0 + h, 0, 0)),
            pl.BlockSpec(blk, lambda b, h: (b, N_HEADS + h0 + h, 0, 0)),
            pl.BlockSpec(blk, lambda b, h: (b, 2 * N_HEADS + h0 + h, 0, 0)),
            pl.BlockSpec((1, 1, 2 * BAND_BLOCK), lambda b, h: (h, 0, 0)),
        ],
        out_specs=[pl.BlockSpec(blk, lambda b, h: (b, h, 0, 0)), pl.BlockSpec(blk, lambda b, h: (b, h, 0, 0))],
        out_shape=[jax.ShapeDtypeStruct((B, H_C, L, r * LANES), BF16),
                   jax.ShapeDtypeStruct((B, H_C, L, r * LANES), F32)],
        compiler_params=_params(("arbitrary", "arbitrary")),
        name=f"dilated_r{r}",
    )(view, view, view, tab)
    return o.reshape(B, H_C, S, LANES), lse.reshape(B, H_C, S, LANES)


def _merge_kernel(oa_ref, ob_ref, o1_ref, o2_ref, o3_ref, l1_ref, l2_ref, l3_ref, g0_ref, g1_ref, g2_ref,
                  wa_ref, wb_ref, wc_ref, wo_ref, x_ref, out_ref):
    heads = []
    for h in range(H_C):
        l1, l2, l3 = l1_ref[0, h], l2_ref[0, h], l3_ref[0, h]
        top = jnp.maximum(jnp.maximum(l1, l2), l3)
        e1, e2, e3 = jnp.exp(l1 - top), jnp.exp(l2 - top), jnp.exp(l3 - top)
        mix = e1 * o1_ref[0, h].astype(F32) + e2 * o2_ref[0, h].astype(F32) + e3 * o3_ref[0, h].astype(F32)
        heads.append((mix / (e1 + e2 + e3)).astype(BF16))
    oc = jnp.concatenate(heads, axis=-1)
    merged = (g0_ref[0].astype(F32) * _dot(oa_ref[0], wa_ref[...])
              + g1_ref[0].astype(F32) * _dot(ob_ref[0], wb_ref[...])
              + g2_ref[0].astype(F32) * _dot(oc, wc_ref[...]))
    out_ref[0] = x_ref[0] + _dot(merged.astype(BF16), wo_ref[...])


def _merge_out(oa, ob, dil, gates, wa, wb, wc, wo, x, *, tm):
    B, S, D = x.shape
    (o1, l1), (o2, l2), (o3, l3) = dil
    rows = lambda w: pl.BlockSpec((1, tm, w), lambda b, s: (b, s, 0))
    heads = pl.BlockSpec((1, H_C, tm, LANES), lambda b, s: (b, 0, s, 0))
    gate = lambda g: pl.BlockSpec((1, tm, D), lambda b, s: (b, s, g))
    resident = lambda w: pl.BlockSpec(w.shape, lambda b, s: (0, 0), pipeline_mode=pl.Buffered(1))
    return pl.pallas_call(
        _merge_kernel,
        grid=(B, S // tm),
        in_specs=[rows(oa.shape[-1]), rows(ob.shape[-1]), heads, heads, heads, heads, heads, heads,
                  gate(0), gate(1), gate(2), resident(wa), resident(wb), resident(wc), resident(wo), rows(D)],
        out_specs=rows(D),
        out_shape=jax.ShapeDtypeStruct((B, S, D), F32),
        compiler_params=_params(("arbitrary", "arbitrary")),
        name="merge_out",
    )(oa, ob, o1, o2, o3, l1, l2, l3, gates, gates, gates, wa, wb, wc, wo, x)


def _ffn_kernel(x_ref, g_ref, wg_ref, wu_ref, wd_ref, o_ref, h_ref, acc_ref):
    f = pl.program_id(2)

    @pl.when(f == 0)
    def _():
        h_ref[...] = (_rms_rows(x_ref[0]) * g_ref[...]).astype(BF16)
        acc_ref[...] = x_ref[0]

    h = h_ref[...]
    gate = _dot(h, wg_ref[...])
    up = _dot(h, wu_ref[...])
    act = (gate * jax.nn.sigmoid(gate) * up).astype(BF16)
    acc_ref[...] += _dot(act, wd_ref[...])

    @pl.when(f == pl.num_programs(2) - 1)
    def _():
        o_ref[0] = acc_ref[...]


def _ffn(x, g, w_gu, w_down, *, tm, tf):
    B, S, D = x.shape
    nf = D_FF // tf
    return pl.pallas_call(
        _ffn_kernel,
        grid=(B, S // tm, nf),
        in_specs=[
            pl.BlockSpec((1, tm, D), lambda b, s, f: (b, s, 0)),
            pl.BlockSpec((1, D), lambda b, s, f: (0, 0)),
            pl.BlockSpec((D, tf), lambda b, s, f: (0, f)),
            pl.BlockSpec((D, tf), lambda b, s, f: (0, nf + f)),
            pl.BlockSpec((tf, D), lambda b, s, f: (f, 0)),
        ],
        out_specs=pl.BlockSpec((1, tm, D), lambda b, s, f: (b, s, 0)),
        out_shape=jax.ShapeDtypeStruct((B, S, D), F32),
        scratch_shapes=[pltpu.VMEM((tm, D), BF16), pltpu.VMEM((tm, D), F32)],
        compiler_params=_params(("arbitrary", "arbitrary", "arbitrary")),
        name="ffn",
    )(x, g, w_gu, w_gu, w_down)


def _t5_bucket(dist):
    max_exact = N_BUCKETS // 2
    d = jnp.maximum(dist, 0)
    df = jnp.maximum(d, 1).astype(F32)
    large = max_exact + (jnp.log(df / max_exact) / math.log(MAX_DISTANCE / max_exact)
                         * (N_BUCKETS - max_exact)).astype(jnp.int32)
    large = jnp.minimum(large, N_BUCKETS - 1)
    return jnp.where(d < max_exact, d, large)


def _bias_tables(rel_bias, S):
    by_dist = rel_bias[_t5_bucket(jnp.arange(S))].T.astype(F32)
    nblk = S // MOBA_BLOCK
    idx = MOBA_BLOCK * jnp.arange(nblk)[:, None] + (MOBA_BLOCK - 1) - jnp.arange(2 * MOBA_BLOCK)[None, :]
    tab_b = jnp.where(idx >= 0, by_dist[:H_B][:, jnp.clip(idx, 0, S - 1)], 0.0)
    tabs_c = []
    for _, r in DILATIONS:
        idx = (BAND_BLOCK - jnp.arange(2 * BAND_BLOCK)) * r
        tabs_c.append(jnp.where(idx >= 0, by_dist[H_B:][:, jnp.clip(idx, 0, S - 1)], 0.0)[:, None, :])
    return tab_b, tabs_c


def _layer(x, g_mix, w_in, q_gain, k_gain, w_branch, w_out, g_ffn, w_gu, w_down, tab_b, tabs_c):
    scale = HEAD_DIM ** -0.5
    ones_a = jnp.ones((H_A, HEAD_DIM), F32)
    colgain = jnp.concatenate([
        (jnp.concatenate([ones_a, q_gain.astype(F32)], axis=0) * scale).reshape(-1),
        jnp.concatenate([ones_a, k_gain.astype(F32)], axis=0).reshape(-1),
        jnp.ones((MIX_WIDTH,), F32)])[None, :]
    w_in_b = w_in.astype(BF16)
    qkv = _qkv_proj(x, g_mix[None, :], w_in_b[:, :3 * MIX_WIDTH], colgain, tm=1024, tn=512)
    gates = _gates_proj(x, g_mix[None, :], w_in_b[:, 3 * MIX_WIDTH:], tm=1024, tn=512)

    oa = _sb_attention(qkv, blk=256)
    ob = _moba_attention(qkv, tab_b)
    dil = [_dilated_pattern(qkv, tab, r) for tab, (_, r) in zip(tabs_c, DILATIONS)]

    wb = w_branch.astype(BF16)
    ea, eb = H_A * HEAD_DIM, (H_A + H_B) * HEAD_DIM
    x = _merge_out(oa, ob, dil, gates, wb[:ea], wb[ea:eb], wb[eb:], w_out.astype(BF16), x, tm=256)
    return _ffn(x, g_ffn[None, :], w_gu.astype(BF16), w_down.astype(BF16), tm=512, tf=512)


def kernel(x, g_mix, w_in, q_gain, k_gain, w_branch, w_out, g_ffn, w_gu, w_down, rel_bias):
    depth = g_mix.shape[0]
    tab_b, tabs_c = _bias_tables(rel_bias, x.shape[1])
    for l in range(depth):
        x = _layer(x, g_mix[l], w_in[l], q_gain[l], k_gain[l], w_branch[l], w_out[l],
                   g_ffn[l], w_gu[l], w_down[l], tab_b, tabs_c)
    return x
```

```python
import functools
import math

import jax
import jax.numpy as jnp
from jax import lax
from jax.experimental import pallas as pl
from jax.experimental.pallas import tpu as pltpu

D_MODEL = 2048
HEAD_DIM = 128
H_A = 4
H_B = 6
H_C = 6
N_HEADS = H_A + H_B + H_C
MIX_WIDTH = N_HEADS * HEAD_DIM
N_BRANCH = 3
D_FF = 5632
MOBA_BLOCK = 256
MOBA_TOPK = 3
DILATIONS = ((128, 1), (512, 4), (2048, 16))
BAND_BLOCK = 128
N_BUCKETS = 32
MAX_DISTANCE = 2048
RMS_EPS = 1e-6
NEG_INF = -1e30

LANES = 128
VMEM_LIMIT = 56 * 1024 * 1024

SB_SKIP_SUM = 110.0

F32 = jnp.float32
BF16 = jnp.bfloat16

SLOTS_MAIN = 3 * (H_A + H_B)
SLOTS_DIL = 3 * H_C
SB_Q, SB_K, SB_V = 0, H_A, 2 * H_A
MB_Q, MB_K, MB_V = 3 * H_A, 3 * H_A + H_B, 3 * H_A + 2 * H_B


def _params(sem):
    return pltpu.CompilerParams(dimension_semantics=sem, vmem_limit_bytes=VMEM_LIMIT)


def _dot(a, b):
    return jnp.dot(a, b, preferred_element_type=F32)


def _dot_nt(a, b):
    return lax.dot_general(a, b, (((1,), (1,)), ((), ())), preferred_element_type=F32)


def _rows(n, size, length=None):
    start = n * size if isinstance(n, int) else pl.multiple_of(n * size, size)
    return pl.ds(start, size if length is None else length)


def _rms_rows(x):
    return x * lax.rsqrt(jnp.mean(x * x, axis=-1, keepdims=True) + RMS_EPS)


PROJ_HEADS = 6
PROJ_TN = PROJ_HEADS * LANES
J_MAIN = SLOTS_MAIN // PROJ_HEADS
J_DIL = SLOTS_DIL // PROJ_HEADS
J_GATE = N_BRANCH * D_MODEL // PROJ_TN
NORMED_TILES = (2, 3, J_MAIN, J_MAIN + 1)


def _proj_kernel(x_ref, g_ref, w_ref, cg_ref, main_ref, d1_ref, d4_ref, d16_ref, gates_ref, h_ref, y_ref):
    j = pl.program_id(2)
    tm = h_ref.shape[0]

    @pl.when(j == 0)
    def _():
        h_ref[...] = (_rms_rows(x_ref[0]) * g_ref[...]).astype(BF16)

    acc = _dot(h_ref[...], w_ref[...])
    is_gate = j >= J_MAIN + J_DIL
    is_norm = functools.reduce(jnp.logical_or, [j == t for t in NORMED_TILES])

    @pl.when(is_gate)
    def _():
        gates_ref[0] = jax.nn.sigmoid(acc).astype(BF16)

    @pl.when(is_norm)
    def _():
        for hh in range(PROJ_HEADS):
            sl = slice(hh * LANES, (hh + 1) * LANES)
            y_ref[hh] = _rms_rows(acc[:, sl]) * cg_ref[:, sl]

    @pl.when(jnp.logical_not(is_gate | is_norm))
    def _():
        for hh in range(PROJ_HEADS):
            sl = slice(hh * LANES, (hh + 1) * LANES)
            y_ref[hh] = acc[:, sl] * cg_ref[:, sl]

    @pl.when(j < J_MAIN)
    def _():
        for hh in range(PROJ_HEADS):
            main_ref[0, hh] = y_ref[hh].astype(BF16)

    @pl.when((j >= J_MAIN) & jnp.logical_not(is_gate))
    def _():
        for hh in range(PROJ_HEADS):
            d1_ref[0, hh] = y_ref[hh].astype(BF16)
            for r, out in ((4, d4_ref), (16, d16_ref)):
                for c in range(r):
                    out[0, hh, :, c * LANES:(c + 1) * LANES] = y_ref[hh, pl.ds(c, tm // r, stride=r), :].astype(BF16)


def _proj(x, g, w, colgain, *, tm):
    B, S, D = x.shape
    tn, hp = PROJ_TN, PROJ_HEADS
    jd, jg = J_MAIN, J_MAIN + J_DIL
    dil = lambda r: pl.BlockSpec((1, hp, tm // r, r * LANES),
                                 lambda b, s, j: (b, jnp.clip(j - jd, 0, J_DIL - 1), s, 0))
    dil_shape = lambda r: jax.ShapeDtypeStruct((B, SLOTS_DIL, S // r, r * LANES), BF16)
    return pl.pallas_call(
        _proj_kernel,
        grid=(B, S // tm, J_MAIN + J_DIL + J_GATE),
        in_specs=[
            pl.BlockSpec((1, tm, D), lambda b, s, j: (b, s, 0)),
            pl.BlockSpec((1, D), lambda b, s, j: (0, 0)),
            pl.BlockSpec((D, tn), lambda b, s, j: (0, j)),
            pl.BlockSpec((1, tn), lambda b, s, j: (0, jnp.minimum(j, jg - 1))),
        ],
        out_specs=[
            pl.BlockSpec((1, hp, tm, LANES), lambda b, s, j: (b, jnp.minimum(j, jd - 1), s, 0)),
            dil(1), dil(4), dil(16),
            pl.BlockSpec((1, tm, tn), lambda b, s, j: (b, s, jnp.maximum(j - jg, 0))),
        ],
        out_shape=[
            jax.ShapeDtypeStruct((B, SLOTS_MAIN, S, LANES), BF16),
            dil_shape(1), dil_shape(4), dil_shape(16),
            jax.ShapeDtypeStruct((B, S, N_BRANCH * D_MODEL), BF16),
        ],
        scratch_shapes=[pltpu.VMEM((tm, D), BF16), pltpu.VMEM((hp, tm, LANES), F32)],
        compiler_params=_params(("arbitrary", "arbitrary", "arbitrary")),
        name="proj",
    )(x, g, w, colgain)


def _softplus(z):
    return jnp.maximum(z, 0.0) + jnp.log(1.0 + jnp.exp(-jnp.abs(z)))


def _sb_kernel(q_ref, k_ref, v_ref, u_ref, o_ref, *, blk):
    i = pl.program_id(2)
    q = q_ref[0, 0]
    u = u_ref[...]
    row = lax.broadcasted_iota(jnp.int32, (blk, blk), 0)
    col = lax.broadcasted_iota(jnp.int32, (blk, blk), 1)
    past = col < row

    def block(j, c, acc, diag):
        start = pl.multiple_of(j * blk, blk)
        k = k_ref[0, 0, pl.ds(start, blk), :]
        v = v_ref[0, 0, pl.ds(start, blk), :]
        z = _dot_nt(q, k)
        sp = _softplus(z)
        spm = jnp.where(past, sp, 0.0) if diag else sp
        hi = spm.astype(BF16)
        lo = (spm - hi.astype(F32)).astype(BF16)
        after = _dot(hi, u) + _dot(lo, u)
        w = jnp.exp(z - sp - after - c)
        if diag:
            w = jnp.where(past, w, 0.0)
        acc = acc + _dot(w.astype(BF16), v)
        c = c + after[:, :1] + spm[:, :1]
        return c, acc

    c, acc = block(i, jnp.zeros((blk, 1), F32), jnp.zeros((blk, HEAD_DIM), F32), True)

    def cond(carry):
        j, c_min, _, _ = carry
        return (j >= 0) & (c_min < SB_SKIP_SUM)

    def body(carry):
        j, _, c, acc = carry
        c, acc = block(j, c, acc, False)
        return j - 1, jnp.min(c), c, acc

    _, _, _, acc = lax.while_loop(cond, body, (i - 1, jnp.min(c), c, acc))
    o_ref[0] = acc.astype(BF16)


def _sb_attention(qkv, *, blk):
    B, _, S, _ = qkv.shape
    tri = (jnp.arange(blk)[:, None] > jnp.arange(blk)[None, :]).astype(BF16)
    return pl.pallas_call(
        functools.partial(_sb_kernel, blk=blk),
        grid=(B, H_A, S // blk),
        in_specs=[
            pl.BlockSpec((1, 1, blk, LANES), lambda b, h, i: (b, SB_Q + h, i, 0)),
            pl.BlockSpec((1, 1, S, LANES), lambda b, h, i: (b, SB_K + h, 0, 0)),
            pl.BlockSpec((1, 1, S, LANES), lambda b, h, i: (b, SB_V + h, 0, 0)),
            pl.BlockSpec((blk, blk), lambda b, h, i: (0, 0)),
        ],
        out_specs=pl.BlockSpec((1, blk, LANES), lambda b, h, i: (b, i, h)),
        out_shape=jax.ShapeDtypeStruct((B, S, H_A * HEAD_DIM), BF16),
        compiler_params=_params(("arbitrary", "arbitrary", "arbitrary")),
        name="sb_attention",
    )(qkv, qkv, qkv, tri)


def _toeplitz(row_vals, n_rows, shift):
    full = jnp.broadcast_to(row_vals, (n_rows, row_vals.shape[1]))
    return pltpu.roll(full, shift, 1, stride=1, stride_axis=0)


def _moba_kernel(q_ref, k_ref, v_ref, tab_ref, o_ref, tb_ref, km_ref, ka_ref, va_ref, *, nblk, n_sel):
    b = pl.program_id(1)
    i = pl.program_id(2)
    blk = MOBA_BLOCK
    wide = 2 * blk

    @pl.when((b == 0) & (i == 0))
    def _():
        for d in range(nblk):
            tb_ref[d] = _toeplitz(tab_ref[0, d:d + 1, :], blk, blk + 1)[:, :blk]

    @pl.when(i == 0)
    def _():
        lane = lax.broadcasted_iota(jnp.int32, (blk, LANES), 1)
        va_ref[:, LANES:] = jnp.ones((va_ref.shape[0], LANES), BF16)
        for n in range(nblk):
            rows = slice(n * blk, (n + 1) * blk)
            kb = k_ref[0, 0, rows, :]
            km_ref[n:n + 1, :] = jnp.mean(kb.astype(F32), axis=0, keepdims=True)
            ka_ref[rows, :LANES] = kb
            ka_ref[rows, LANES:] = jnp.where(lane == n, 1.0, 0.0).astype(BF16)
            va_ref[rows, :LANES] = v_ref[0, 0, rows, :]

    q = q_ref[0, 0]
    blk_id = lax.broadcasted_iota(jnp.int32, (nblk, blk), 0)
    fully_past = blk_id < i
    gate = jnp.where(fully_past, _dot_nt(km_ref[...].astype(BF16), q), NEG_INF)
    sel = jnp.zeros((nblk, blk), F32)
    for _ in range(n_sel):
        top = jnp.max(gate, axis=0, keepdims=True)
        first = jnp.min(jnp.where(gate == top, blk_id, nblk), axis=0, keepdims=True)
        pick = blk_id == first
        sel = jnp.where(pick, 1.0, sel)
        gate = jnp.where(pick, -jnp.inf, gate)
    pen_t = jnp.where(fully_past & (sel > 0.0), 0.0, NEG_INF).astype(BF16)
    pen_t = jnp.concatenate([pen_t, jnp.zeros((LANES - nblk, blk), BF16)], axis=0)
    eye = (lax.broadcasted_iota(jnp.int32, (blk, blk), 0) == lax.broadcasted_iota(jnp.int32, (blk, blk), 1))
    pen = _dot_nt(jnp.where(eye, 1.0, 0.0).astype(BF16), pen_t)
    qa = jnp.concatenate([q, pen.astype(BF16)], axis=1)

    row = lax.broadcasted_iota(jnp.int32, (blk, blk), 0)
    col = lax.broadcasted_iota(jnp.int32, (blk, blk), 1)
    own = pl.ds(pl.multiple_of(i * blk, blk), blk)
    s = jnp.where(col <= row, _dot_nt(q, ka_ref[own, :LANES]) + tb_ref[0], NEG_INF)
    m = jnp.max(s, axis=-1, keepdims=True)
    accl = _dot(jnp.exp(s - m).astype(BF16), va_ref[own, :])

    def body(t, carry):
        m, accl = carry
        keys = pl.ds(pl.multiple_of(t * wide, wide), wide)
        s = _dot_nt(qa, ka_ref[keys, :])
        d = i - 2 * t
        s0 = s[:, :blk] + tb_ref[d]
        s1 = s[:, blk:] + tb_ref[d - 1]
        m_new = jnp.maximum(m, jnp.maximum(jnp.max(s0, axis=-1, keepdims=True), jnp.max(s1, axis=-1, keepdims=True)))
        p = jnp.concatenate([jnp.exp(s0 - m_new), jnp.exp(s1 - m_new)], axis=1).astype(BF16)
        accl = jnp.exp(m - m_new) * accl + _dot(p, va_ref[keys, :])
        return m_new, accl

    m, accl = lax.fori_loop(0, (i + 1) // 2, body, (m, accl))
    o_ref[0] = (accl[:, :LANES] / accl[:, LANES:]).astype(BF16)


def _moba_attention(qkv, tab):
    B, _, S, _ = qkv.shape
    blk = MOBA_BLOCK
    nblk = S // blk
    assert nblk % 2 == 0 and nblk <= LANES
    n_sel = min(MOBA_TOPK, nblk - 1)
    return pl.pallas_call(
        functools.partial(_moba_kernel, nblk=nblk, n_sel=n_sel),
        grid=(H_B, B, nblk),
        in_specs=[
            pl.BlockSpec((1, 1, blk, LANES), lambda h, b, i: (b, MB_Q + h, i, 0)),
            pl.BlockSpec((1, 1, S, LANES), lambda h, b, i: (b, MB_K + h, 0, 0)),
            pl.BlockSpec((1, 1, S, LANES), lambda h, b, i: (b, MB_V + h, 0, 0)),
            pl.BlockSpec((1, nblk, 2 * blk), lambda h, b, i: (h, 0, 0)),
        ],
        out_specs=pl.BlockSpec((1, blk, LANES), lambda h, b, i: (b, i, h)),
        out_shape=jax.ShapeDtypeStruct((B, S, H_B * HEAD_DIM), BF16),
        scratch_shapes=[pltpu.VMEM((nblk, blk, blk), F32), pltpu.VMEM((nblk, LANES), F32),
                        pltpu.VMEM((S, 2 * LANES), BF16), pltpu.VMEM((S, 2 * LANES), BF16)],
        compiler_params=_params(("arbitrary", "arbitrary", "arbitrary")),
        name="moba_attention",
    )(qkv, qkv, qkv, tab)


def _dilated_kernel(q1_ref, k1_ref, v1_ref, q4_ref, k4_ref, v4_ref, q16_ref, k16_ref, v16_ref, tab_ref,
                    o_ref, o4_ref, l4_ref, o16_ref, l16_ref, *, seq):
    bb = BAND_BLOCK
    qi = lax.broadcasted_iota(jnp.int32, (bb, 2 * bb), 0)
    kj = lax.broadcasted_iota(jnp.int32, (bb, 2 * bb), 1)
    visible = (kj >= qi) & (kj <= qi + bb)

    def pattern(g, r, q_ref, k_ref, v_ref, emit):
        bias = jnp.where(visible, _toeplitz(tab_ref[0, g:g + 1, :], bb, 0), NEG_INF)
        nb = seq // r // bb
        for c in range(r):
            cs = slice(c * LANES, (c + 1) * LANES)

            def attend(n, s, v):
                m = jnp.max(s, axis=-1, keepdims=True)
                p = jnp.exp(s - m)
                den = jnp.sum(p, axis=-1, keepdims=True)
                emit(c, n, _dot(p.astype(BF16), v) / den, m + jnp.log(den))

            attend(0, _dot_nt(q_ref[0, 0, 0:bb, cs], k_ref[0, 0, 0:bb, cs]) + bias[:, bb:], v_ref[0, 0, 0:bb, cs])

            def body(n, _):
                keys = _rows(n - 1, bb, 2 * bb)
                attend(n, _dot_nt(q_ref[0, 0, _rows(n, bb), cs], k_ref[0, 0, keys, cs]) + bias, v_ref[0, 0, keys, cs])
                return 0

            lax.fori_loop(1, nb, body, 0)

    def to_scratch(r, o_scr, l_scr):
        def emit(c, n, o, lse):
            window = _rows(n, bb * r)
            o_scr.at[window][pl.ds(c, bb, stride=r), :] = o
            l_scr.at[window][pl.ds(c, bb, stride=r), :] = jnp.broadcast_to(lse, (bb, LANES))
        return emit

    pattern(2, 16, q16_ref, k16_ref, v16_ref, to_scratch(16, o16_ref, l16_ref))
    pattern(1, 4, q4_ref, k4_ref, v4_ref, to_scratch(4, o4_ref, l4_ref))

    def merge(c, n, o1, l1):
        rows = _rows(n, bb)
        l4, l16 = l4_ref[rows, :], l16_ref[rows, :]
        top = jnp.maximum(jnp.maximum(l1, l4), l16)
        e1, e4, e16 = jnp.exp(l1 - top), jnp.exp(l4 - top), jnp.exp(l16 - top)
        mix = e1 * o1 + e4 * o4_ref[rows, :] + e16 * o16_ref[rows, :]
        o_ref[0, rows, :] = (mix / (e1 + e4 + e16)).astype(BF16)

    pattern(0, 1, q1_ref, k1_ref, v1_ref, merge)


def _dilated_attention(d1, d4, d16, tabs):
    B, _, S, _ = d1.shape
    assert [r for _, r in DILATIONS] == [1, 4, 16] and all(w // r == BAND_BLOCK for w, r in DILATIONS)
    head = lambda r, part: pl.BlockSpec((1, 1, S // r, r * LANES), lambda b, h: (b, part * H_C + h, 0, 0))
    return pl.pallas_call(
        functools.partial(_dilated_kernel, seq=S),
        grid=(B, H_C),
        in_specs=[head(r, part) for r in (1, 4, 16) for part in range(3)]
        + [pl.BlockSpec((1, len(DILATIONS), 2 * BAND_BLOCK), lambda b, h: (h, 0, 0))],
        out_specs=pl.BlockSpec((1, S, LANES), lambda b, h: (b, 0, h)),
        out_shape=jax.ShapeDtypeStruct((B, S, H_C * HEAD_DIM), BF16),
        scratch_shapes=[pltpu.VMEM((S, LANES), F32)] * 4,
        compiler_params=_params(("arbitrary", "arbitrary")),
        name="dilated_attention",
    )(d1, d1, d1, d4, d4, d4, d16, d16, d16, tabs)


def _merge_kernel(oa_ref, ob_ref, oc_ref, g0_ref, g1_ref, g2_ref, wa_ref, wb_ref, wc_ref, wo_ref, x_ref, out_ref):
    merged = (g0_ref[0].astype(F32) * _dot(oa_ref[0], wa_ref[...])
              + g1_ref[0].astype(F32) * _dot(ob_ref[0], wb_ref[...])
              + g2_ref[0].astype(F32) * _dot(oc_ref[0], wc_ref[...]))
    out_ref[0] = x_ref[0] + _dot(merged.astype(BF16), wo_ref[...])


def _merge_out(oa, ob, oc, gates, wa, wb, wc, wo, x, *, tm):
    B, S, D = x.shape
    rows = lambda w: pl.BlockSpec((1, tm, w), lambda b, s: (b, s, 0))
    gate = lambda g: pl.BlockSpec((1, tm, D), lambda b, s: (b, s, g))
    resident = lambda w: pl.BlockSpec(w.shape, lambda b, s: (0, 0), pipeline_mode=pl.Buffered(1))
    return pl.pallas_call(
        _merge_kernel,
        grid=(B, S // tm),
        in_specs=[rows(oa.shape[-1]), rows(ob.shape[-1]), rows(oc.shape[-1]), gate(0), gate(1), gate(2),
                  resident(wa), resident(wb), resident(wc), resident(wo), rows(D)],
        out_specs=rows(D),
        out_shape=jax.ShapeDtypeStruct((B, S, D), F32),
        compiler_params=_params(("arbitrary", "arbitrary")),
        name="merge_out",
    )(oa, ob, oc, gates, gates, gates, wa, wb, wc, wo, x)


def _ffn_kernel(x_ref, g_ref, wg_ref, wu_ref, wd_ref, o_ref, h_ref, acc_ref):
    f = pl.program_id(2)

    @pl.when(f == 0)
    def _():
        h_ref[...] = (_rms_rows(x_ref[0]) * g_ref[...]).astype(BF16)
        acc_ref[...] = x_ref[0]

    h = h_ref[...]
    gate = _dot(h, wg_ref[...])
    up = _dot(h, wu_ref[...])
    act = (gate * jax.nn.sigmoid(gate) * up).astype(BF16)
    acc_ref[...] += _dot(act, wd_ref[...])

    @pl.when(f == pl.num_programs(2) - 1)
    def _():
        o_ref[0] = acc_ref[...]


def _ffn(x, g, w_gu, w_down, *, tm, tf):
    B, S, D = x.shape
    nf = D_FF // tf
    return pl.pallas_call(
        _ffn_kernel,
        grid=(B, S // tm, nf),
        in_specs=[
            pl.BlockSpec((1, tm, D), lambda b, s, f: (b, s, 0)),
            pl.BlockSpec((1, D), lambda b, s, f: (0, 0)),
            pl.BlockSpec((D, tf), lambda b, s, f: (0, f)),
            pl.BlockSpec((D, tf), lambda b, s, f: (0, nf + f)),
            pl.BlockSpec((tf, D), lambda b, s, f: (f, 0)),
        ],
        out_specs=pl.BlockSpec((1, tm, D), lambda b, s, f: (b, s, 0)),
        out_shape=jax.ShapeDtypeStruct((B, S, D), F32),
        scratch_shapes=[pltpu.VMEM((tm, D), BF16), pltpu.VMEM((tm, D), F32)],
        compiler_params=_params(("arbitrary", "arbitrary", "arbitrary")),
        name="ffn",
    )(x, g, w_gu, w_gu, w_down)


def _t5_bucket(dist):
    max_exact = N_BUCKETS // 2
    d = jnp.maximum(dist, 0)
    df = jnp.maximum(d, 1).astype(F32)
    large = max_exact + (jnp.log(df / max_exact) / math.log(MAX_DISTANCE / max_exact)
                         * (N_BUCKETS - max_exact)).astype(jnp.int32)
    large = jnp.minimum(large, N_BUCKETS - 1)
    return jnp.where(d < max_exact, d, large)


def _bias_tables(rel_bias, S):
    by_dist = rel_bias[_t5_bucket(jnp.arange(S))].T.astype(F32)
    nblk = S // MOBA_BLOCK
    idx = MOBA_BLOCK * jnp.arange(nblk)[:, None] + (MOBA_BLOCK - 1) - jnp.arange(2 * MOBA_BLOCK)[None, :]
    tab_b = jnp.where(idx >= 0, by_dist[:H_B][:, jnp.clip(idx, 0, S - 1)], 0.0)
    tabs_c = []
    for _, r in DILATIONS:
        idx = (BAND_BLOCK - jnp.arange(2 * BAND_BLOCK)) * r
        tabs_c.append(jnp.where(idx >= 0, by_dist[H_B:][:, jnp.clip(idx, 0, S - 1)], 0.0))
    return tab_b, jnp.stack(tabs_c, axis=1)


def _proj_weights(w_in, q_gain, k_gain):
    scale = HEAD_DIM ** -0.5
    ones = lambda n: jnp.ones((n * HEAD_DIM,), F32)
    hb = H_A + H_B

    def cols(part, h0, h1):
        return w_in[:, part * MIX_WIDTH + h0 * HEAD_DIM: part * MIX_WIDTH + h1 * HEAD_DIM]

    qg, kg = q_gain.astype(F32) * scale, k_gain.astype(F32)
    w = jnp.concatenate(
        [cols(p, 0, H_A) for p in range(3)] + [cols(p, H_A, hb) for p in range(3)]
        + [cols(p, hb, N_HEADS) for p in range(3)] + [w_in[:, 3 * MIX_WIDTH:]], axis=1).astype(BF16)
    colgain = jnp.concatenate([
        ones(H_A) * scale, ones(H_A), ones(H_A),
        qg[:H_B].reshape(-1), kg[:H_B].reshape(-1), ones(H_B),
        qg[H_B:].reshape(-1), kg[H_B:].reshape(-1), ones(H_C)])[None, :]
    return w, colgain


def _layer(x, g_mix, w_in, q_gain, k_gain, w_branch, w_out, g_ffn, w_gu, w_down, tab_b, tabs_c):
    w, colgain = _proj_weights(w_in, q_gain, k_gain)
    qkv, d1, d4, d16, gates = _proj(x, g_mix[None, :], w, colgain, tm=1024)

    oa = _sb_attention(qkv, blk=256)
    ob = _moba_attention(qkv, tab_b)
    oc = _dilated_attention(d1, d4, d16, tabs_c)

    wb = w_branch.astype(BF16)
    ea, eb = H_A * HEAD_DIM, (H_A + H_B) * HEAD_DIM
    x = _merge_out(oa, ob, oc, gates, wb[:ea], wb[ea:eb], wb[eb:], w_out.astype(BF16), x, tm=256)
    return _ffn(x, g_ffn[None, :], w_gu.astype(BF16), w_down.astype(BF16), tm=512, tf=512)


def kernel(x, g_mix, w_in, q_gain, k_gain, w_branch, w_out, g_ffn, w_gu, w_down, rel_bias):
    depth = g_mix.shape[0]
    tab_b, tabs_c = _bias_tables(rel_bias, x.shape[1])
    for l in range(depth):
        x = _layer(x, g_mix[l], w_in[l], q_gain[l], k_gain[l], w_branch[l], w_out[l],
                   g_ffn[l], w_gu[l], w_down[l], tab_b, tabs_c)
    return x
```

```python
import functools
import math

import jax
import jax.numpy as jnp
from jax import lax
from jax.experimental import pallas as pl
from jax.experimental.pallas import tpu as pltpu

D_MODEL = 2048
HEAD_DIM = 128
H_A = 4
H_B = 6
H_C = 6
N_HEADS = H_A + H_B + H_C
MIX_WIDTH = N_HEADS * HEAD_DIM
N_BRANCH = 3
D_FF = 5632
MOBA_BLOCK = 256
MOBA_TOPK = 3
DILATIONS = ((128, 1), (512, 4), (2048, 16))
BAND_BLOCK = 128
N_BUCKETS = 32
MAX_DISTANCE = 2048
RMS_EPS = 1e-6
NEG_INF = -1e30

LANES = 128
VMEM_LIMIT = 56 * 1024 * 1024

SB_SKIP_SUM = 110.0

F32 = jnp.float32
BF16 = jnp.bfloat16

SLOTS_MAIN = 3 * (H_A + H_B)
SLOTS_DIL = 3 * H_C
SB_Q, SB_K, SB_V = 0, H_A, 2 * H_A
MB_Q, MB_K, MB_V = 3 * H_A, 3 * H_A + H_B, 3 * H_A + 2 * H_B


def _params(sem):
    return pltpu.CompilerParams(dimension_semantics=sem, vmem_limit_bytes=VMEM_LIMIT)


def _dot(a, b):
    return jnp.dot(a, b, preferred_element_type=F32)


def _dot_nt(a, b):
    return lax.dot_general(a, b, (((1,), (1,)), ((), ())), preferred_element_type=F32)


def _rows(n, size, length=None):
    start = n * size if isinstance(n, int) else pl.multiple_of(n * size, size)
    return pl.ds(start, size if length is None else length)


def _rms_rows(x):
    return x * lax.rsqrt(jnp.mean(x * x, axis=-1, keepdims=True) + RMS_EPS)


PROJ_HEADS = 6
PROJ_TN = PROJ_HEADS * LANES
J_MAIN = SLOTS_MAIN // PROJ_HEADS
J_DIL = SLOTS_DIL // PROJ_HEADS
J_GATE = N_BRANCH * D_MODEL // PROJ_TN
NORMED_TILES = (2, 3, J_MAIN, J_MAIN + 1)
PROJ_SUB = 256


def _proj_kernel(x_ref, g_ref, w_ref, cg_ref, main_ref, d1_ref, d4_ref, d16_ref, gates_ref, h_ref, y_ref):
    j = pl.program_id(2)
    tm = h_ref.shape[0]
    sub = PROJ_SUB

    @pl.when(j == 0)
    def _():
        h_ref[...] = (_rms_rows(x_ref[0]) * g_ref[...]).astype(BF16)

    is_gate = j >= J_MAIN + J_DIL
    is_dil = (j >= J_MAIN) & jnp.logical_not(is_gate)
    is_norm = functools.reduce(jnp.logical_or, [j == t for t in NORMED_TILES])

    def sweep(epilogue):
        for mi in range(tm // sub):
            rows = slice(mi * sub, (mi + 1) * sub)
            epilogue(mi, rows, _dot(h_ref[rows, :], w_ref[...]))

    def heads(acc, normed):
        for hh in range(PROJ_HEADS):
            sl = slice(hh * LANES, (hh + 1) * LANES)
            yield hh, (_rms_rows(acc[:, sl]) if normed else acc[:, sl]) * cg_ref[:, sl]

    @pl.when(is_gate)
    def _():
        def epilogue(mi, rows, acc):
            gates_ref[0, rows, :] = jax.nn.sigmoid(acc).astype(BF16)
        sweep(epilogue)

    for normed in (False, True):
        @pl.when((j < J_MAIN) & (is_norm == normed))
        def _():
            def epilogue(mi, rows, acc):
                for hh, y in heads(acc, normed):
                    main_ref[0, hh, rows, :] = y.astype(BF16)
            sweep(epilogue)

        @pl.when(is_dil & (is_norm == normed))
        def _():
            def epilogue(mi, rows, acc):
                for hh, y in heads(acc, normed):
                    d1_ref[0, hh, 0, rows, :] = y.astype(BF16)
                    y_ref[mi % 2, hh] = y
                    for r, out in ((4, d4_ref), (16, d16_ref)):
                        for c in range(r):
                            out[0, hh, c, mi * (sub // r):(mi + 1) * (sub // r), :] = (
                                y_ref[mi % 2, hh, pl.ds(c, sub // r, stride=r), :].astype(BF16))
            sweep(epilogue)


def _proj(x, g, w, colgain, *, tm):
    B, S, D = x.shape
    tn, hp = PROJ_TN, PROJ_HEADS
    jd, jg = J_MAIN, J_MAIN + J_DIL
    dil = lambda r: pl.BlockSpec((1, hp, r, tm // r, LANES),
                                 lambda b, s, j: (b, jnp.clip(j - jd, 0, J_DIL - 1), 0, s, 0))
    dil_shape = lambda r: jax.ShapeDtypeStruct((B, SLOTS_DIL, r, S // r, LANES), BF16)
    return pl.pallas_call(
        _proj_kernel,
        grid=(B, S // tm, J_MAIN + J_DIL + J_GATE),
        in_specs=[
            pl.BlockSpec((1, tm, D), lambda b, s, j: (b, s, 0)),
            pl.BlockSpec((1, D), lambda b, s, j: (0, 0)),
            pl.BlockSpec((D, tn), lambda b, s, j: (0, j)),
            pl.BlockSpec((1, tn), lambda b, s, j: (0, jnp.minimum(j, jg - 1))),
        ],
        out_specs=[
            pl.BlockSpec((1, hp, tm, LANES), lambda b, s, j: (b, jnp.minimum(j, jd - 1), s, 0)),
            dil(1), dil(4), dil(16),
            pl.BlockSpec((1, tm, tn), lambda b, s, j: (b, s, jnp.maximum(j - jg, 0))),
        ],
        out_shape=[
            jax.ShapeDtypeStruct((B, SLOTS_MAIN, S, LANES), BF16),
            dil_shape(1), dil_shape(4), dil_shape(16),
            jax.ShapeDtypeStruct((B, S, N_BRANCH * D_MODEL), BF16),
        ],
        scratch_shapes=[pltpu.VMEM((tm, D), BF16), pltpu.VMEM((2, hp, PROJ_SUB, LANES), F32)],
        compiler_params=_params(("arbitrary", "arbitrary", "arbitrary")),
        name="proj",
    )(x, g, w, colgain)


def _softplus(z):
    return jnp.maximum(z, 0.0) + jnp.log(1.0 + jnp.exp(-jnp.abs(z)))


def _sb_kernel(q_ref, k_ref, v_ref, u_ref, o_ref, *, blk):
    i = pl.program_id(2)
    q = q_ref[0, 0]
    u = u_ref[...]
    row = lax.broadcasted_iota(jnp.int32, (blk, blk), 0)
    col = lax.broadcasted_iota(jnp.int32, (blk, blk), 1)
    past = col < row

    def block(j, c, acc, diag):
        start = pl.multiple_of(j * blk, blk)
        k = k_ref[0, 0, pl.ds(start, blk), :]
        v = v_ref[0, 0, pl.ds(start, blk), :]
        z = _dot_nt(q, k)
        sp = _softplus(z)
        spm = jnp.where(past, sp, 0.0) if diag else sp
        hi = spm.astype(BF16)
        lo = (spm - hi.astype(F32)).astype(BF16)
        after = _dot(hi, u) + _dot(lo, u)
        w = jnp.exp(z - sp - after - c)
        if diag:
            w = jnp.where(past, w, 0.0)
        acc = acc + _dot(w.astype(BF16), v)
        c = c + after[:, :1] + spm[:, :1]
        return c, acc

    c, acc = block(i, jnp.zeros((blk, 1), F32), jnp.zeros((blk, HEAD_DIM), F32), True)

    def cond(carry):
        j, c_min, _, _ = carry
        return (j >= 0) & (c_min < SB_SKIP_SUM)

    def body(carry):
        j, _, c, acc = carry
        c, acc = block(j, c, acc, False)
        return j - 1, jnp.min(c), c, acc

    _, _, _, acc = lax.while_loop(cond, body, (i - 1, jnp.min(c), c, acc))
    o_ref[0] = acc.astype(BF16)


def _sb_attention(qkv, *, blk):
    B, _, S, _ = qkv.shape
    tri = (jnp.arange(blk)[:, None] > jnp.arange(blk)[None, :]).astype(BF16)
    return pl.pallas_call(
        functools.partial(_sb_kernel, blk=blk),
        grid=(B, H_A, S // blk),
        in_specs=[
            pl.BlockSpec((1, 1, blk, LANES), lambda b, h, i: (b, SB_Q + h, i, 0)),
            pl.BlockSpec((1, 1, S, LANES), lambda b, h, i: (b, SB_K + h, 0, 0)),
            pl.BlockSpec((1, 1, S, LANES), lambda b, h, i: (b, SB_V + h, 0, 0)),
            pl.BlockSpec((blk, blk), lambda b, h, i: (0, 0)),
        ],
        out_specs=pl.BlockSpec((1, blk, LANES), lambda b, h, i: (b, i, h)),
        out_shape=jax.ShapeDtypeStruct((B, S, H_A * HEAD_DIM), BF16),
        compiler_params=_params(("arbitrary", "arbitrary", "arbitrary")),
        name="sb_attention",
    )(qkv, qkv, qkv, tri)


def _toeplitz(row_vals, n_rows, shift):
    full = jnp.broadcast_to(row_vals, (n_rows, row_vals.shape[1]))
    return pltpu.roll(full, shift, 1, stride=1, stride_axis=0)


def _moba_kernel(q_ref, k_ref, v_ref, tab_ref, o_ref, tb_ref, ka_ref, va_ref, qa_ref, m_ref, accl_ref, *, nblk, n_sel):
    blk = MOBA_BLOCK
    wide = 2 * blk
    npair = nblk // 2

    @pl.when(pl.program_id(1) == 0)
    def _():
        for d in range(nblk):
            tb_ref[d] = _toeplitz(tab_ref[0, d:d + 1, :], blk, blk + 1)[:, :blk]

    lane = lax.broadcasted_iota(jnp.int32, (blk, LANES), 1)
    va_ref[:, LANES:] = jnp.ones((va_ref.shape[0], LANES), BF16)
    means = []
    for n in range(nblk):
        rows = slice(n * blk, (n + 1) * blk)
        kb = k_ref[0, 0, rows, :]
        means.append(jnp.mean(kb.astype(F32), axis=0, keepdims=True))
        ka_ref[rows, :LANES] = kb
        ka_ref[rows, LANES:] = jnp.where(lane == n, 1.0, 0.0).astype(BF16)
        va_ref[rows, :LANES] = v_ref[0, 0, rows, :]
    km = jnp.concatenate(means, axis=0).astype(BF16)

    blk_id = lax.broadcasted_iota(jnp.int32, (nblk, blk), 0)
    row = lax.broadcasted_iota(jnp.int32, (blk, blk), 0)
    col = lax.broadcasted_iota(jnp.int32, (blk, blk), 1)
    eye = jnp.where(row == col, 1.0, 0.0).astype(BF16)
    causal = col <= row

    def rows_of(i0, par):
        return pl.ds(pl.multiple_of(i0 * blk, wide) + par * blk, blk)

    def gating(u, _):
        for par in range(2):
            i = 2 * u + par
            rows = rows_of(2 * u, par)
            q = q_ref[0, 0, rows, :]
            fully_past = blk_id < i
            gate = jnp.where(fully_past, _dot_nt(km, q), NEG_INF)
            sel = jnp.zeros((nblk, blk), F32)
            for _ in range(n_sel):
                top = jnp.max(gate, axis=0, keepdims=True)
                first = jnp.min(jnp.where(gate == top, blk_id, nblk), axis=0, keepdims=True)
                pick = blk_id == first
                sel = jnp.where(pick, 1.0, sel)
                gate = jnp.where(pick, -jnp.inf, gate)
            visible = (blk_id == i) | (fully_past & (sel > 0.0))
            pen_t = jnp.where(visible, 0.0, NEG_INF).astype(BF16)
            pen_t = jnp.concatenate([pen_t, jnp.zeros((LANES - nblk, blk), BF16)], axis=0)
            qa_ref[rows, :LANES] = q
            qa_ref[rows, LANES:] = _dot_nt(eye, pen_t).astype(BF16)
        return 0

    lax.fori_loop(0, npair, gating, 0)

    def softmax_pv(s0, s1, m_old, vals):
        m = jnp.maximum(jnp.max(s0, axis=-1, keepdims=True), jnp.max(s1, axis=-1, keepdims=True))
        if m_old is not None:
            m = jnp.maximum(m, m_old)
        p = jnp.concatenate([jnp.exp(s0 - m), jnp.exp(s1 - m)], axis=1).astype(BF16)
        return m, _dot(p, vals)

    def own(u, _):
        keys = _rows(u, wide)
        kc, vc = ka_ref[keys, :], va_ref[keys, :]
        for par in range(2):
            rows = rows_of(2 * u, par)
            s = _dot_nt(qa_ref[rows, :], kc)
            if par == 0:
                s0, s1 = jnp.where(causal, s[:, :blk] + tb_ref[0], NEG_INF), s[:, blk:]
            else:
                s0, s1 = s[:, :blk] + tb_ref[1], jnp.where(causal, s[:, blk:] + tb_ref[0], NEG_INF)
            m_ref[rows, :], accl_ref[rows, :] = softmax_pv(s0, s1, None, vc)
        return 0

    lax.fori_loop(0, npair, own, 0)

    def past_chunk(t, _):
        keys = _rows(t, wide)

        def pair(j, _):
            i0 = 2 * (t + 1 + j)
            kc, vc = ka_ref[keys, :], va_ref[keys, :]
            windows = [rows_of(i0, par) for par in range(2)]
            state = [(m_ref[rows, :], accl_ref[rows, :]) for rows in windows]
            new = []
            for par, (rows, (m_old, accl)) in enumerate(zip(windows, state)):
                d = i0 + par - 2 * t
                s = _dot_nt(qa_ref[rows, :], kc)
                m, pv = softmax_pv(s[:, :blk] + tb_ref[d], s[:, blk:] + tb_ref[d - 1], m_old, vc)
                new.append((m, jnp.exp(m_old - m) * accl + pv))
            for rows, (m, accl) in zip(windows, new):
                m_ref[rows, :] = m
                accl_ref[rows, :] = accl
            return 0

        lax.fori_loop(0, npair - 1 - t, pair, 0)
        return 0

    lax.fori_loop(0, npair - 1, past_chunk, 0)

    def finish(i, _):
        rows = _rows(i, blk)
        o_ref[0, rows, :] = (accl_ref[rows, :LANES] / accl_ref[rows, LANES:]).astype(BF16)
        return 0

    lax.fori_loop(0, nblk, finish, 0)


def _moba_attention(qkv, tab):
    B, _, S, _ = qkv.shape
    blk = MOBA_BLOCK
    nblk = S // blk
    assert nblk % 2 == 0 and nblk <= LANES
    n_sel = min(MOBA_TOPK, nblk - 1)
    head = lambda slot: pl.BlockSpec((1, 1, S, LANES), lambda h, b: (b, slot + h, 0, 0))
    return pl.pallas_call(
        functools.partial(_moba_kernel, nblk=nblk, n_sel=n_sel),
        grid=(H_B, B),
        in_specs=[head(MB_Q), head(MB_K), head(MB_V), pl.BlockSpec((1, nblk, 2 * blk), lambda h, b: (h, 0, 0))],
        out_specs=pl.BlockSpec((1, S, LANES), lambda h, b: (b, 0, h)),
        out_shape=jax.ShapeDtypeStruct((B, S, H_B * HEAD_DIM), BF16),
        scratch_shapes=[pltpu.VMEM((nblk, blk, blk), F32),
                        pltpu.VMEM((S, 2 * LANES), BF16), pltpu.VMEM((S, 2 * LANES), BF16),
                        pltpu.VMEM((S, 2 * LANES), BF16), pltpu.VMEM((S, 1), F32),
                        pltpu.VMEM((S, 2 * LANES), F32)],
        compiler_params=_params(("arbitrary", "arbitrary")),
        name="moba_attention",
    )(qkv, qkv, qkv, tab)


DIL_UNROLL = 8


def _dilated_kernel(q1_ref, k1_ref, v1_ref, q4_ref, k4_ref, v4_ref, q16_ref, k16_ref, v16_ref, tab_ref, o_ref,
                    c4o_ref, c4l_ref, c16o_ref, c16l_ref, s4o_ref, s4l_ref, s16o_ref, s16l_ref, *, seq):
    bb = BAND_BLOCK
    qi = lax.broadcasted_iota(jnp.int32, (bb, 2 * bb), 0)
    kj = lax.broadcasted_iota(jnp.int32, (bb, 2 * bb), 1)
    visible = (kj >= qi) & (kj <= qi + bb)

    def pattern(g, r, q_ref, k_ref, v_ref, emit):
        bias = jnp.where(visible, _toeplitz(tab_ref[0, g:g + 1, :], bb, 0), NEG_INF)
        bias_first = jnp.concatenate([bias[:, bb:], jnp.full((bb, bb), NEG_INF, F32)], axis=1)
        nb = seq // r // bb

        def one(u):
            c, n = u // nb, u % nb
            keys = _rows(jnp.maximum(n - 1, 0), bb, 2 * bb)
            s = _dot_nt(q_ref[0, 0, c, _rows(n, bb), :], k_ref[0, 0, c, keys, :]) + jnp.where(n == 0, bias_first, bias)
            m = jnp.max(s, axis=-1, keepdims=True)
            p = jnp.exp(s - m)
            den = jnp.sum(p, axis=-1, keepdims=True)
            emit(c, n, _dot(p.astype(BF16), v_ref[0, 0, c, keys, :]) / den, m + jnp.log(den))

        def body(t, _):
            for k in range(DIL_UNROLL):
                one(t * DIL_UNROLL + k)
            return 0

        lax.fori_loop(0, r * nb // DIL_UNROLL, body, 0)

    def class_major(o_cm, l_cm):
        def emit(c, n, o, lse):
            o_cm[c, _rows(n, bb), :] = o
            l_cm[c, _rows(n, bb), :] = jnp.broadcast_to(lse, (bb, LANES))
        return emit

    def to_sequence(r, cm, sq):
        for c in range(r):
            sq[pl.ds(c, seq // r, stride=r), :] = cm[c]

    pattern(2, 16, q16_ref, k16_ref, v16_ref, class_major(c16o_ref, c16l_ref))
    to_sequence(16, c16o_ref, s16o_ref)
    to_sequence(16, c16l_ref, s16l_ref)
    pattern(1, 4, q4_ref, k4_ref, v4_ref, class_major(c4o_ref, c4l_ref))
    to_sequence(4, c4o_ref, s4o_ref)
    to_sequence(4, c4l_ref, s4l_ref)

    def merge(c, n, o1, l1):
        rows = _rows(n, bb)
        l4, l16 = s4l_ref[rows, :], s16l_ref[rows, :]
        top = jnp.maximum(jnp.maximum(l1, l4), l16)
        e1, e4, e16 = jnp.exp(l1 - top), jnp.exp(l4 - top), jnp.exp(l16 - top)
        mix = e1 * o1 + e4 * s4o_ref[rows, :] + e16 * s16o_ref[rows, :]
        o_ref[0, rows, :] = (mix / (e1 + e4 + e16)).astype(BF16)

    pattern(0, 1, q1_ref, k1_ref, v1_ref, merge)


def _dilated_attention(d1, d4, d16, tabs):
    B, S = d1.shape[0], d1.shape[3]
    assert [r for _, r in DILATIONS] == [1, 4, 16] and all(w // r == BAND_BLOCK for w, r in DILATIONS)
    assert (S // BAND_BLOCK) % DIL_UNROLL == 0
    head = lambda r, part: pl.BlockSpec((1, 1, r, S // r, LANES), lambda b, h: (b, part * H_C + h, 0, 0, 0))
    class_major = lambda r: pltpu.VMEM((r, S // r, LANES), F32)
    return pl.pallas_call(
        functools.partial(_dilated_kernel, seq=S),
        grid=(B, H_C),
        in_specs=[head(r, part) for r in (1, 4, 16) for part in range(3)]
        + [pl.BlockSpec((1, len(DILATIONS), 2 * BAND_BLOCK), lambda b, h: (h, 0, 0))],
        out_specs=pl.BlockSpec((1, S, LANES), lambda b, h: (b, 0, h)),
        out_shape=jax.ShapeDtypeStruct((B, S, H_C * HEAD_DIM), BF16),
        scratch_shapes=[class_major(4), class_major(4), class_major(16), class_major(16)]
        + [pltpu.VMEM((S, LANES), F32)] * 4,
        compiler_params=_params(("arbitrary", "arbitrary")),
        name="dilated_attention",
    )(d1, d1, d1, d4, d4, d4, d16, d16, d16, tabs)


def _merge_kernel(oa_ref, ob_ref, oc_ref, g0_ref, g1_ref, g2_ref, wa_ref, wb_ref, wc_ref, wo_ref, x_ref, out_ref):
    merged = (g0_ref[0].astype(F32) * _dot(oa_ref[0], wa_ref[...])
              + g1_ref[0].astype(F32) * _dot(ob_ref[0], wb_ref[...])
              + g2_ref[0].astype(F32) * _dot(oc_ref[0], wc_ref[...]))
    out_ref[0] = x_ref[0] + _dot(merged.astype(BF16), wo_ref[...])


def _merge_out(oa, ob, oc, gates, wa, wb, wc, wo, x, *, tm):
    B, S, D = x.shape
    rows = lambda w: pl.BlockSpec((1, tm, w), lambda b, s: (b, s, 0))
    gate = lambda g: pl.BlockSpec((1, tm, D), lambda b, s: (b, s, g))
    resident = lambda w: pl.BlockSpec(w.shape, lambda b, s: (0, 0), pipeline_mode=pl.Buffered(1))
    return pl.pallas_call(
        _merge_kernel,
        grid=(B, S // tm),
        in_specs=[rows(oa.shape[-1]), rows(ob.shape[-1]), rows(oc.shape[-1]), gate(0), gate(1), gate(2),
                  resident(wa), resident(wb), resident(wc), resident(wo), rows(D)],
        out_specs=rows(D),
        out_shape=jax.ShapeDtypeStruct((B, S, D), F32),
        compiler_params=_params(("arbitrary", "arbitrary")),
        name="merge_out",
    )(oa, ob, oc, gates, gates, gates, wa, wb, wc, wo, x)


def _ffn_kernel(x_ref, g_ref, wg_ref, wu_ref, wd_ref, o_ref, h_ref, acc_ref):
    f = pl.program_id(2)

    @pl.when(f == 0)
    def _():
        h_ref[...] = (_rms_rows(x_ref[0]) * g_ref[...]).astype(BF16)
        acc_ref[...] = x_ref[0]

    h = h_ref[...]
    gate = _dot(h, wg_ref[...])
    up = _dot(h, wu_ref[...])
    act = (gate * jax.nn.sigmoid(gate) * up).astype(BF16)
    acc_ref[...] += _dot(act, wd_ref[...])

    @pl.when(f == pl.num_programs(2) - 1)
    def _():
        o_ref[0] = acc_ref[...]


def _ffn(x, g, w_gu, w_down, *, tm, tf):
    B, S, D = x.shape
    nf = D_FF // tf
    return pl.pallas_call(
        _ffn_kernel,
        grid=(B, S // tm, nf),
        in_specs=[
            pl.BlockSpec((1, tm, D), lambda b, s, f: (b, s, 0)),
            pl.BlockSpec((1, D), lambda b, s, f: (0, 0)),
            pl.BlockSpec((D, tf), lambda b, s, f: (0, f)),
            pl.BlockSpec((D, tf), lambda b, s, f: (0, nf + f)),
            pl.BlockSpec((tf, D), lambda b, s, f: (f, 0)),
        ],
        out_specs=pl.BlockSpec((1, tm, D), lambda b, s, f: (b, s, 0)),
        out_shape=jax.ShapeDtypeStruct((B, S, D), F32),
        scratch_shapes=[pltpu.VMEM((tm, D), BF16), pltpu.VMEM((tm, D), F32)],
        compiler_params=_params(("arbitrary", "arbitrary", "arbitrary")),
        name="ffn",
    )(x, g, w_gu, w_gu, w_down)


def _t5_bucket(dist):
    max_exact = N_BUCKETS // 2
    d = jnp.maximum(dist, 0)
    df = jnp.maximum(d, 1).astype(F32)
    large = max_exact + (jnp.log(df / max_exact) / math.log(MAX_DISTANCE / max_exact)
                         * (N_BUCKETS - max_exact)).astype(jnp.int32)
    large = jnp.minimum(large, N_BUCKETS - 1)
    return jnp.where(d < max_exact, d, large)


def _bias_tables(rel_bias, S):
    by_dist = rel_bias[_t5_bucket(jnp.arange(S))].T.astype(F32)
    nblk = S // MOBA_BLOCK
    idx = MOBA_BLOCK * jnp.arange(nblk)[:, None] + (MOBA_BLOCK - 1) - jnp.arange(2 * MOBA_BLOCK)[None, :]
    tab_b = jnp.where(idx >= 0, by_dist[:H_B][:, jnp.clip(idx, 0, S - 1)], 0.0)
    tabs_c = []
    for _, r in DILATIONS:
        idx = (BAND_BLOCK - jnp.arange(2 * BAND_BLOCK)) * r
        tabs_c.append(jnp.where(idx >= 0, by_dist[H_B:][:, jnp.clip(idx, 0, S - 1)], 0.0))
    return tab_b, jnp.stack(tabs_c, axis=1)


def _proj_weights(w_in, q_gain, k_gain):
    scale = HEAD_DIM ** -0.5
    ones = lambda n: jnp.ones((n * HEAD_DIM,), F32)
    hb = H_A + H_B

    def cols(part, h0, h1):
        return w_in[:, part * MIX_WIDTH + h0 * HEAD_DIM: part * MIX_WIDTH + h1 * HEAD_DIM]

    qg, kg = q_gain.astype(F32) * scale, k_gain.astype(F32)
    w = jnp.concatenate(
        [cols(p, 0, H_A) for p in range(3)] + [cols(p, H_A, hb) for p in range(3)]
        + [cols(p, hb, N_HEADS) for p in range(3)] + [w_in[:, 3 * MIX_WIDTH:]], axis=1).astype(BF16)
    colgain = jnp.concatenate([
        ones(H_A) * scale, ones(H_A), ones(H_A),
        qg[:H_B].reshape(-1), kg[:H_B].reshape(-1), ones(H_B),
        qg[H_B:].reshape(-1), kg[H_B:].reshape(-1), ones(H_C)])[None, :]
    return w, colgain


def _layer(x, g_mix, w_in, q_gain, k_gain, w_branch, w_out, g_ffn, w_gu, w_down, tab_b, tabs_c):
    w, colgain = _proj_weights(w_in, q_gain, k_gain)
    qkv, d1, d4, d16, gates = _proj(x, g_mix[None, :], w, colgain, tm=1024)

    oa = _sb_attention(qkv, blk=256)
    ob = _moba_attention(qkv, tab_b)
    oc = _dilated_attention(d1, d4, d16, tabs_c)

    wb = w_branch.astype(BF16)
    ea, eb = H_A * HEAD_DIM, (H_A + H_B) * HEAD_DIM
    x = _merge_out(oa, ob, oc, gates, wb[:ea], wb[ea:eb], wb[eb:], w_out.astype(BF16), x, tm=256)
    return _ffn(x, g_ffn[None, :], w_gu.astype(BF16), w_down.astype(BF16), tm=512, tf=512)


def kernel(x, g_mix, w_in, q_gain, k_gain, w_branch, w_out, g_ffn, w_gu, w_down, rel_bias):
    depth = g_mix.shape[0]
    tab_b, tabs_c = _bias_tables(rel_bias, x.shape[1])
    for l in range(depth):
        x = _layer(x, g_mix[l], w_in[l], q_gain[l], k_gain[l], w_branch[l], w_out[l],
                   g_ffn[l], w_gu[l], w_down[l], tab_b, tabs_c)
    return x
```

```python
import functools
import math

import jax
import jax.numpy as jnp
from jax import lax
from jax.experimental import pallas as pl
from jax.experimental.pallas import tpu as pltpu

D_MODEL = 2048
HEAD_DIM = 128
H_A = 4
H_B = 6
H_C = 6
N_HEADS = H_A + H_B + H_C
MIX_WIDTH = N_HEADS * HEAD_DIM
N_BRANCH = 3
D_FF = 5632
MOBA_BLOCK = 256
MOBA_TOPK = 3
DILATIONS = ((128, 1), (512, 4), (2048, 16))
BAND_BLOCK = 128
N_BUCKETS = 32
MAX_DISTANCE = 2048
RMS_EPS = 1e-6
NEG_INF = -1e30

LANES = 128
VMEM_LIMIT = 56 * 1024 * 1024

SB_SKIP_SUM = 110.0

F32 = jnp.float32
BF16 = jnp.bfloat16

SLOTS_MAIN = 3 * (H_A + H_B)
SLOTS_DIL = 3 * H_C
SB_Q, SB_K, SB_V = 0, H_A, 2 * H_A
MB_Q, MB_K, MB_V = 3 * H_A, 3 * H_A + H_B, 3 * H_A + 2 * H_B


def _params(sem):
    return pltpu.CompilerParams(dimension_semantics=sem, vmem_limit_bytes=VMEM_LIMIT)


def _dot(a, b):
    return jnp.dot(a, b, preferred_element_type=F32)


def _dot_nt(a, b):
    return lax.dot_general(a, b, (((1,), (1,)), ((), ())), preferred_element_type=F32)


def _rows(n, size, length=None):
    start = n * size if isinstance(n, int) else pl.multiple_of(n * size, size)
    return pl.ds(start, size if length is None else length)


def _rms_rows(x):
    return x * lax.rsqrt(jnp.mean(x * x, axis=-1, keepdims=True) + RMS_EPS)


PROJ_HEADS = 6
PROJ_TN = PROJ_HEADS * LANES
J_MAIN = SLOTS_MAIN // PROJ_HEADS
J_DIL = SLOTS_DIL // PROJ_HEADS
J_GATE = N_BRANCH * D_MODEL // PROJ_TN
NORMED_TILES = (2, 3, J_MAIN, J_MAIN + 1)
PROJ_SUB = 256


def _proj_kernel(x_ref, g_ref, w_ref, cg_ref, main_ref, d1_ref, d4_ref, d16_ref, gates_ref, h_ref, y_ref):
    j = pl.program_id(2)
    tm = h_ref.shape[0]
    sub = PROJ_SUB

    @pl.when(j == 0)
    def _():
        h_ref[...] = (_rms_rows(x_ref[0]) * g_ref[...]).astype(BF16)

    is_gate = j >= J_MAIN + J_DIL
    is_dil = (j >= J_MAIN) & jnp.logical_not(is_gate)
    is_norm = functools.reduce(jnp.logical_or, [j == t for t in NORMED_TILES])

    def sweep(epilogue):
        for mi in range(tm // sub):
            rows = slice(mi * sub, (mi + 1) * sub)
            epilogue(mi, rows, _dot(h_ref[rows, :], w_ref[...]))

    def heads(acc, normed):
        for hh in range(PROJ_HEADS):
            sl = slice(hh * LANES, (hh + 1) * LANES)
            yield hh, (_rms_rows(acc[:, sl]) if normed else acc[:, sl]) * cg_ref[:, sl]

    @pl.when(is_gate)
    def _():
        def epilogue(mi, rows, acc):
            gates_ref[0, rows, :] = jax.nn.sigmoid(acc).astype(BF16)
        sweep(epilogue)

    for normed in (False, True):
        @pl.when((j < J_MAIN) & (is_norm == normed))
        def _():
            def epilogue(mi, rows, acc):
                for hh, y in heads(acc, normed):
                    main_ref[0, hh, rows, :] = y.astype(BF16)
            sweep(epilogue)

        @pl.when(is_dil & (is_norm == normed))
        def _():
            def epilogue(mi, rows, acc):
                for hh, y in heads(acc, normed):
                    d1_ref[0, hh, 0, rows, :] = y.astype(BF16)
                    y_ref[mi % 2, hh] = y
                    for r, out in ((4, d4_ref), (16, d16_ref)):
                        for c in range(r):
                            out[0, hh, c, mi * (sub // r):(mi + 1) * (sub // r), :] = (
                                y_ref[mi % 2, hh, pl.ds(c, sub // r, stride=r), :].astype(BF16))
            sweep(epilogue)


def _proj(x, g, w, colgain, *, tm):
    B, S, D = x.shape
    tn, hp = PROJ_TN, PROJ_HEADS
    jd, jg = J_MAIN, J_MAIN + J_DIL
    dil = lambda r: pl.BlockSpec((1, hp, r, tm // r, LANES),
                                 lambda b, s, j: (b, jnp.clip(j - jd, 0, J_DIL - 1), 0, s, 0))
    dil_shape = lambda r: jax.ShapeDtypeStruct((B, SLOTS_DIL, r, S // r, LANES), BF16)
    return pl.pallas_call(
        _proj_kernel,
        grid=(B, S // tm, J_MAIN + J_DIL + J_GATE),
        in_specs=[
            pl.BlockSpec((1, tm, D), lambda b, s, j: (b, s, 0)),
            pl.BlockSpec((1, D), lambda b, s, j: (0, 0)),
            pl.BlockSpec((D, tn), lambda b, s, j: (0, j)),
            pl.BlockSpec((1, tn), lambda b, s, j: (0, jnp.minimum(j, jg - 1))),
        ],
        out_specs=[
            pl.BlockSpec((1, hp, tm, LANES), lambda b, s, j: (b, jnp.minimum(j, jd - 1), s, 0)),
            dil(1), dil(4), dil(16),
            pl.BlockSpec((1, tm, tn), lambda b, s, j: (b, s, jnp.maximum(j - jg, 0))),
        ],
        out_shape=[
            jax.ShapeDtypeStruct((B, SLOTS_MAIN, S, LANES), BF16),
            dil_shape(1), dil_shape(4), dil_shape(16),
            jax.ShapeDtypeStruct((B, S, N_BRANCH * D_MODEL), BF16),
        ],
        scratch_shapes=[pltpu.VMEM((tm, D), BF16), pltpu.VMEM((2, hp, PROJ_SUB, LANES), F32)],
        compiler_params=_params(("arbitrary", "arbitrary", "arbitrary")),
        name="proj",
    )(x, g, w, colgain)


def _softplus(z):
    return jnp.maximum(z, 0.0) + jnp.log(1.0 + jnp.exp(-jnp.abs(z)))


def _sb_kernel(q_ref, k_ref, v_ref, u_ref, o_ref, *, blk):
    i = pl.program_id(1)
    u = u_ref[...]
    row = lax.broadcasted_iota(jnp.int32, (blk, blk), 0)
    col = lax.broadcasted_iota(jnp.int32, (blk, blk), 1)
    past = col < row

    def blocks(j, state, diag):
        keys = pl.ds(pl.multiple_of(j * blk, blk), blk)
        heads = range(H_A)
        z = [_dot_nt(q_ref[0, h], k_ref[0, h, keys, :]) for h in heads]
        sp = [_softplus(z[h]) for h in heads]
        spm = [jnp.where(past, sp[h], 0.0) for h in heads] if diag else sp
        hi = [spm[h].astype(BF16) for h in heads]
        lo = [(spm[h] - hi[h].astype(F32)).astype(BF16) for h in heads]
        after = [_dot(hi[h], u) + _dot(lo[h], u) for h in heads]
        w = [jnp.exp(z[h] - sp[h] - after[h] - state[h][0]) for h in heads]
        if diag:
            w = [jnp.where(past, w[h], 0.0) for h in heads]
        acc = [state[h][1] + _dot(w[h].astype(BF16), v_ref[0, h, keys, :]) for h in heads]
        c = [state[h][0] + after[h][:, :1] + spm[h][:, :1] for h in heads]
        return tuple(zip(c, acc))

    def smallest(state):
        return functools.reduce(jnp.minimum, [jnp.min(c) for c, _ in state])

    zeros = (jnp.zeros((blk, 1), F32), jnp.zeros((blk, HEAD_DIM), F32))
    state = blocks(i, (zeros,) * H_A, True)

    def cond(carry):
        j, c_min, _ = carry
        return (j >= 0) & (c_min < SB_SKIP_SUM)

    def body(carry):
        j, _, state = carry
        state = blocks(j, state, False)
        return j - 1, smallest(state), state

    _, _, state = lax.while_loop(cond, body, (i - 1, smallest(state), state))
    for h, (_, acc) in enumerate(state):
        o_ref[0, :, h * LANES:(h + 1) * LANES] = acc.astype(BF16)


def _sb_attention(qkv, *, blk):
    B, _, S, _ = qkv.shape
    tri = (jnp.arange(blk)[:, None] > jnp.arange(blk)[None, :]).astype(BF16)
    return pl.pallas_call(
        functools.partial(_sb_kernel, blk=blk),
        grid=(B, S // blk),
        in_specs=[
            pl.BlockSpec((1, H_A, blk, LANES), lambda b, i: (b, SB_Q // H_A, i, 0)),
            pl.BlockSpec((1, H_A, S, LANES), lambda b, i: (b, SB_K // H_A, 0, 0)),
            pl.BlockSpec((1, H_A, S, LANES), lambda b, i: (b, SB_V // H_A, 0, 0)),
            pl.BlockSpec((blk, blk), lambda b, i: (0, 0)),
        ],
        out_specs=pl.BlockSpec((1, blk, H_A * LANES), lambda b, i: (b, i, 0)),
        out_shape=jax.ShapeDtypeStruct((B, S, H_A * HEAD_DIM), BF16),
        compiler_params=_params(("arbitrary", "arbitrary")),
        name="sb_attention",
    )(qkv, qkv, qkv, tri)


def _toeplitz(row_vals, n_rows, shift):
    full = jnp.broadcast_to(row_vals, (n_rows, row_vals.shape[1]))
    return pltpu.roll(full, shift, 1, stride=1, stride_axis=0)


MOBA_STRIP = 64
GATE_UNROLL = 4


def _moba_kernel(q_ref, k_ref, v_ref, tab_ref, o_ref, tb_ref, ka_ref, va_ref, qa_ref, m_ref, accl_ref, sc_ref, p_ref,
                 al_ref, *, nblk, n_sel):
    blk = MOBA_BLOCK
    wide = 2 * blk
    npair = nblk // 2

    @pl.when(pl.program_id(1) == 0)
    def _():
        for d in range(nblk):
            tb_ref[d] = _toeplitz(tab_ref[0, d:d + 1, :], blk, blk + 1)[:, :blk]

    lane = lax.broadcasted_iota(jnp.int32, (blk, LANES), 1)
    va_ref[:, LANES:] = jnp.ones((va_ref.shape[0], LANES), BF16)
    means = []
    for n in range(nblk):
        rows = slice(n * blk, (n + 1) * blk)
        kb = k_ref[0, 0, rows, :]
        means.append(jnp.mean(kb.astype(F32), axis=0, keepdims=True))
        ka_ref[rows, :LANES] = kb
        ka_ref[rows, LANES:] = jnp.where(lane == n, 1.0, 0.0).astype(BF16)
        va_ref[rows, :LANES] = v_ref[0, 0, rows, :]
    km = jnp.concatenate(means, axis=0).astype(BF16)

    blk_id = lax.broadcasted_iota(jnp.int32, (nblk, blk), 0)
    row = lax.broadcasted_iota(jnp.int32, (blk, blk), 0)
    col = lax.broadcasted_iota(jnp.int32, (blk, blk), 1)
    eye = jnp.where(row == col, 1.0, 0.0).astype(BF16)
    causal = col <= row

    def gating(g, _):
        ids = [g * GATE_UNROLL + k for k in range(GATE_UNROLL)]
        windows = [pl.ds(pl.multiple_of(g * (GATE_UNROLL * blk), GATE_UNROLL * blk) + k * blk, blk)
                   for k in range(GATE_UNROLL)]
        q = [q_ref[0, 0, rows, :] for rows in windows]
        fully_past = [blk_id < i for i in ids]
        gate = [jnp.where(fp, _dot_nt(km, x), NEG_INF) for fp, x in zip(fully_past, q)]
        sel = [jnp.zeros((nblk, blk), F32) for _ in ids]
        for _ in range(n_sel):
            top = [jnp.max(x, axis=0, keepdims=True) for x in gate]
            first = [jnp.min(jnp.where(x == t, blk_id, nblk), axis=0, keepdims=True)
                     for x, t in zip(gate, top)]
            pick = [blk_id == f for f in first]
            sel = [jnp.where(pk, 1.0, x) for pk, x in zip(pick, sel)]
            gate = [jnp.where(pk, -jnp.inf, x) for pk, x in zip(pick, gate)]
        pad = jnp.zeros((LANES - nblk, blk), BF16)
        pen_t = [jnp.concatenate([jnp.where((blk_id == i) | (fp & (x > 0.0)), 0.0, NEG_INF).astype(BF16), pad], axis=0)
                 for i, fp, x in zip(ids, fully_past, sel)]
        pen = [_dot_nt(eye, x).astype(BF16) for x in pen_t]
        for rows, x, y in zip(windows, q, pen):
            qa_ref[rows, :LANES] = x
            qa_ref[rows, LANES:] = y
        return 0

    lax.fori_loop(0, nblk // GATE_UNROLL, gating, 0)


    def aligned(x, m):
        return x if isinstance(x, int) else pl.multiple_of(x, m)

    def soften(t, j, slot):
        s_base = aligned(j * wide, wide)
        q_base = s_base if t is None else aligned((t + 1 + j) * wide, wide)
        for par in range(2):
            for st in range(blk // MOBA_STRIP):
                tile = slice(st * MOBA_STRIP, (st + 1) * MOBA_STRIP)
                off = par * blk + st * MOBA_STRIP
                s = sc_ref[pl.ds(s_base + off, MOBA_STRIP), :]
                state = pl.ds(q_base + off, MOBA_STRIP)
                if t is None:
                    inside = jnp.where(causal[tile], s[:, par * blk:(par + 1) * blk] + tb_ref[0, tile, :], NEG_INF)
                    other = s[:, blk:] if par == 0 else s[:, :blk] + tb_ref[1, tile, :]
                    s0, s1 = (inside, other) if par == 0 else (other, inside)
                    m = jnp.max(jnp.maximum(s0, s1), axis=-1, keepdims=True)
                else:
                    d = 2 + 2 * j + par
                    s0 = s[:, :blk] + tb_ref[d, tile, :]
                    s1 = s[:, blk:] + tb_ref[d - 1, tile, :]
                    m_old = m_ref[state, :]
                    m = jnp.maximum(m_old, jnp.max(jnp.maximum(s0, s1), axis=-1, keepdims=True))
                    al_ref[slot, off:off + MOBA_STRIP, :] = jnp.exp(m_old - m)
                p_ref[slot, off:off + MOBA_STRIP, :blk] = jnp.exp(s0 - m).astype(BF16)
                p_ref[slot, off:off + MOBA_STRIP, blk:] = jnp.exp(s1 - m).astype(BF16)
                m_ref[state, :] = m

    def accumulate(t, j, slot):
        if t is None:
            rows = _rows(j, wide)
            accl_ref[rows, :] = _dot(p_ref[slot], va_ref[rows, :])
        else:
            rows = _rows(t + 1 + j, wide)
            accl_ref[rows, :] = al_ref[slot] * accl_ref[rows, :] + _dot(p_ref[slot], va_ref[t * wide:(t + 1) * wide, :])

    def phase(t, n):
        soften(t, 0, 0)

        def body(j, _):
            accumulate(t, j - 1, (j - 1) % 2)
            soften(t, j, j % 2)
            return 0

        lax.fori_loop(1, n, body, 0)
        accumulate(t, n - 1, (n - 1) % 2)

    for u in range(npair):
        rows = slice(u * wide, (u + 1) * wide)
        sc_ref[rows, :] = _dot_nt(qa_ref[rows, :], ka_ref[rows, :])
    phase(None, npair)
    for t in range(npair - 1):
        n = npair - 1 - t
        sc_ref[0:n * wide, :] = _dot_nt(qa_ref[(t + 1) * wide:, :], ka_ref[t * wide:(t + 1) * wide, :])
        phase(t, n)


    def finish(i, _):
        rows = _rows(i, blk)
        o_ref[0, rows, :] = (accl_ref[rows, :LANES] / accl_ref[rows, LANES:]).astype(BF16)
        return 0

    lax.fori_loop(0, nblk, finish, 0)


def _moba_attention(qkv, tab):
    B, _, S, _ = qkv.shape
    blk = MOBA_BLOCK
    nblk = S // blk
    assert nblk % 2 == 0 and nblk % GATE_UNROLL == 0 and nblk <= LANES
    n_sel = min(MOBA_TOPK, nblk - 1)
    head = lambda slot: pl.BlockSpec((1, 1, S, LANES), lambda h, b: (b, slot + h, 0, 0))
    return pl.pallas_call(
        functools.partial(_moba_kernel, nblk=nblk, n_sel=n_sel),
        grid=(H_B, B),
        in_specs=[head(MB_Q), head(MB_K), head(MB_V), pl.BlockSpec((1, nblk, 2 * blk), lambda h, b: (h, 0, 0))],
        out_specs=pl.BlockSpec((1, S, LANES), lambda h, b: (b, 0, h)),
        out_shape=jax.ShapeDtypeStruct((B, S, H_B * HEAD_DIM), BF16),
        scratch_shapes=[pltpu.VMEM((nblk, blk, blk), F32),
                        pltpu.VMEM((S, 2 * LANES), BF16), pltpu.VMEM((S, 2 * LANES), BF16),
                        pltpu.VMEM((S, 2 * LANES), BF16), pltpu.VMEM((S, 1), F32),
                        pltpu.VMEM((S, 2 * LANES), F32), pltpu.VMEM((S, 2 * blk), F32),
                        pltpu.VMEM((2, 2 * blk, 2 * blk), BF16), pltpu.VMEM((2, 2 * blk, 1), F32)],
        compiler_params=_params(("arbitrary", "arbitrary")),
        name="moba_attention",
    )(qkv, qkv, qkv, tab)


DIL_UNROLL = 8


def _dilated_kernel(q1_ref, k1_ref, v1_ref, q4_ref, k4_ref, v4_ref, q16_ref, k16_ref, v16_ref, tab_ref, o_ref,
                    c4o_ref, c4l_ref, c16o_ref, c16l_ref, s4o_ref, s4l_ref, s16o_ref, s16l_ref, *, seq):
    bb = BAND_BLOCK
    qi = lax.broadcasted_iota(jnp.int32, (bb, 2 * bb), 0)
    kj = lax.broadcasted_iota(jnp.int32, (bb, 2 * bb), 1)
    visible = (kj >= qi) & (kj <= qi + bb)

    def pattern(g, r, q_ref, k_ref, v_ref, emit):
        bias = jnp.where(visible, _toeplitz(tab_ref[0, g:g + 1, :], bb, 0), NEG_INF)
        bias_first = jnp.concatenate([bias[:, bb:], jnp.full((bb, bb), NEG_INF, F32)], axis=1)
        nb = seq // r // bb

        def body(t, _):
            blocks = [((t * DIL_UNROLL + k) // nb, (t * DIL_UNROLL + k) % nb) for k in range(DIL_UNROLL)]
            keys = [_rows(jnp.maximum(n - 1, 0), bb, 2 * bb) for _, n in blocks]
            s = [_dot_nt(q_ref[0, 0, c, _rows(n, bb), :], k_ref[0, 0, c, kw, :]) + jnp.where(n == 0, bias_first, bias)
                 for (c, n), kw in zip(blocks, keys)]
            m = [jnp.max(x, axis=-1, keepdims=True) for x in s]
            p = [jnp.exp(x - mx) for x, mx in zip(s, m)]
            den = [jnp.sum(x, axis=-1, keepdims=True) for x in p]
            o = [_dot(x.astype(BF16), v_ref[0, 0, c, kw, :]) for x, (c, _), kw in zip(p, blocks, keys)]
            for (c, n), ox, dx, mx in zip(blocks, o, den, m):
                emit(c, n, ox / dx, mx + jnp.log(dx))
            return 0

        lax.fori_loop(0, r * nb // DIL_UNROLL, body, 0)

    def class_major(o_cm, l_cm):
        def emit(c, n, o, lse):
            o_cm[c, _rows(n, bb), :] = o
            l_cm[c, _rows(n, bb), :] = jnp.broadcast_to(lse, (bb, LANES))
        return emit

    def to_sequence(r, cm, sq):
        for c in range(r):
            sq[pl.ds(c, seq // r, stride=r), :] = cm[c]

    pattern(2, 16, q16_ref, k16_ref, v16_ref, class_major(c16o_ref, c16l_ref))
    to_sequence(16, c16o_ref, s16o_ref)
    to_sequence(16, c16l_ref, s16l_ref)
    pattern(1, 4, q4_ref, k4_ref, v4_ref, class_major(c4o_ref, c4l_ref))
    to_sequence(4, c4o_ref, s4o_ref)
    to_sequence(4, c4l_ref, s4l_ref)

    def merge(c, n, o1, l1):
        rows = _rows(n, bb)
        l4, l16 = s4l_ref[rows, :], s16l_ref[rows, :]
        top = jnp.maximum(jnp.maximum(l1, l4), l16)
        e1, e4, e16 = jnp.exp(l1 - top), jnp.exp(l4 - top), jnp.exp(l16 - top)
        mix = e1 * o1 + e4 * s4o_ref[rows, :] + e16 * s16o_ref[rows, :]
        o_ref[0, rows, :] = (mix / (e1 + e4 + e16)).astype(BF16)

    pattern(0, 1, q1_ref, k1_ref, v1_ref, merge)


def _dilated_attention(d1, d4, d16, tabs):
    B, S = d1.shape[0], d1.shape[3]
    assert [r for _, r in DILATIONS] == [1, 4, 16] and all(w // r == BAND_BLOCK for w, r in DILATIONS)
    assert (S // BAND_BLOCK) % DIL_UNROLL == 0
    head = lambda r, part: pl.BlockSpec((1, 1, r, S // r, LANES), lambda b, h: (b, part * H_C + h, 0, 0, 0))
    class_major = lambda r: pltpu.VMEM((r, S // r, LANES), F32)
    return pl.pallas_call(
        functools.partial(_dilated_kernel, seq=S),
        grid=(B, H_C),
        in_specs=[head(r, part) for r in (1, 4, 16) for part in range(3)]
        + [pl.BlockSpec((1, len(DILATIONS), 2 * BAND_BLOCK), lambda b, h: (h, 0, 0))],
        out_specs=pl.BlockSpec((1, S, LANES), lambda b, h: (b, 0, h)),
        out_shape=jax.ShapeDtypeStruct((B, S, H_C * HEAD_DIM), BF16),
        scratch_shapes=[class_major(4), class_major(4), class_major(16), class_major(16)]
        + [pltpu.VMEM((S, LANES), F32)] * 4,
        compiler_params=_params(("arbitrary", "arbitrary")),
        name="dilated_attention",
    )(d1, d1, d1, d4, d4, d4, d16, d16, d16, tabs)


def _merge_kernel(oa_ref, ob_ref, oc_ref, g0_ref, g1_ref, g2_ref, wa_ref, wb_ref, wc_ref, wo_ref, x_ref, out_ref):
    merged = (g0_ref[0].astype(F32) * _dot(oa_ref[0], wa_ref[...])
              + g1_ref[0].astype(F32) * _dot(ob_ref[0], wb_ref[...])
              + g2_ref[0].astype(F32) * _dot(oc_ref[0], wc_ref[...]))
    out_ref[0] = x_ref[0] + _dot(merged.astype(BF16), wo_ref[...])


def _merge_out(oa, ob, oc, gates, wa, wb, wc, wo, x, *, tm):
    B, S, D = x.shape
    rows = lambda w: pl.BlockSpec((1, tm, w), lambda b, s: (b, s, 0))
    gate = lambda g: pl.BlockSpec((1, tm, D), lambda b, s: (b, s, g))
    resident = lambda w: pl.BlockSpec(w.shape, lambda b, s: (0, 0), pipeline_mode=pl.Buffered(1))
    return pl.pallas_call(
        _merge_kernel,
        grid=(B, S // tm),
        in_specs=[rows(oa.shape[-1]), rows(ob.shape[-1]), rows(oc.shape[-1]), gate(0), gate(1), gate(2),
                  resident(wa), resident(wb), resident(wc), resident(wo), rows(D)],
        out_specs=rows(D),
        out_shape=jax.ShapeDtypeStruct((B, S, D), F32),
        compiler_params=_params(("arbitrary", "arbitrary")),
        name="merge_out",
    )(oa, ob, oc, gates, gates, gates, wa, wb, wc, wo, x)


def _ffn_kernel(x_ref, g_ref, wg_ref, wu_ref, wd_ref, o_ref, h_ref, acc_ref):
    f = pl.program_id(2)

    @pl.when(f == 0)
    def _():
        h_ref[...] = (_rms_rows(x_ref[0]) * g_ref[...]).astype(BF16)
        acc_ref[...] = x_ref[0]

    h = h_ref[...]
    gate = _dot(h, wg_ref[...])
    up = _dot(h, wu_ref[...])
    act = (gate * jax.nn.sigmoid(gate) * up).astype(BF16)
    acc_ref[...] += _dot(act, wd_ref[...])

    @pl.when(f == pl.num_programs(2) - 1)
    def _():
        o_ref[0] = acc_ref[...]


def _ffn(x, g, w_gu, w_down, *, tm, tf):
    B, S, D = x.shape
    nf = D_FF // tf
    return pl.pallas_call(
        _ffn_kernel,
        grid=(B, S // tm, nf),
        in_specs=[
            pl.BlockSpec((1, tm, D), lambda b, s, f: (b, s, 0)),
            pl.BlockSpec((1, D), lambda b, s, f: (0, 0)),
            pl.BlockSpec((D, tf), lambda b, s, f: (0, f)),
            pl.BlockSpec((D, tf), lambda b, s, f: (0, nf + f)),
            pl.BlockSpec((tf, D), lambda b, s, f: (f, 0)),
        ],
        out_specs=pl.BlockSpec((1, tm, D), lambda b, s, f: (b, s, 0)),
        out_shape=jax.ShapeDtypeStruct((B, S, D), F32),
        scratch_shapes=[pltpu.VMEM((tm, D), BF16), pltpu.VMEM((tm, D), F32)],
        compiler_params=_params(("arbitrary", "arbitrary", "arbitrary")),
        name="ffn",
    )(x, g, w_gu, w_gu, w_down)


def _t5_bucket(dist):
    max_exact = N_BUCKETS // 2
    d = jnp.maximum(dist, 0)
    df = jnp.maximum(d, 1).astype(F32)
    large = max_exact + (jnp.log(df / max_exact) / math.log(MAX_DISTANCE / max_exact)
                         * (N_BUCKETS - max_exact)).astype(jnp.int32)
    large = jnp.minimum(large, N_BUCKETS - 1)
    return jnp.where(d < max_exact, d, large)


def _bias_tables(rel_bias, S):
    by_dist = rel_bias[_t5_bucket(jnp.arange(S))].T.astype(F32)
    nblk = S // MOBA_BLOCK
    idx = MOBA_BLOCK * jnp.arange(nblk)[:, None] + (MOBA_BLOCK - 1) - jnp.arange(2 * MOBA_BLOCK)[None, :]
    tab_b = jnp.where(idx >= 0, by_dist[:H_B][:, jnp.clip(idx, 0, S - 1)], 0.0)
    tabs_c = []
    for _, r in DILATIONS:
        idx = (BAND_BLOCK - jnp.arange(2 * BAND_BLOCK)) * r
        tabs_c.append(jnp.where(idx >= 0, by_dist[H_B:][:, jnp.clip(idx, 0, S - 1)], 0.0))
    return tab_b, jnp.stack(tabs_c, axis=1)


def _proj_weights(w_in, q_gain, k_gain):
    scale = HEAD_DIM ** -0.5
    ones = lambda n: jnp.ones((n * HEAD_DIM,), F32)
    hb = H_A + H_B

    def cols(part, h0, h1):
        return w_in[:, part * MIX_WIDTH + h0 * HEAD_DIM: part * MIX_WIDTH + h1 * HEAD_DIM]

    qg, kg = q_gain.astype(F32) * scale, k_gain.astype(F32)
    w = jnp.concatenate(
        [cols(p, 0, H_A) for p in range(3)] + [cols(p, H_A, hb) for p in range(3)]
        + [cols(p, hb, N_HEADS) for p in range(3)] + [w_in[:, 3 * MIX_WIDTH:]], axis=1).astype(BF16)
    colgain = jnp.concatenate([
        ones(H_A) * scale, ones(H_A), ones(H_A),
        qg[:H_B].reshape(-1), kg[:H_B].reshape(-1), ones(H_B),
        qg[H_B:].reshape(-1), kg[H_B:].reshape(-1), ones(H_C)])[None, :]
    return w, colgain


def _layer(x, g_mix, w_in, q_gain, k_gain, w_branch, w_out, g_ffn, w_gu, w_down, tab_b, tabs_c):
    w, colgain = _proj_weights(w_in, q_gain, k_gain)
    qkv, d1, d4, d16, gates = _proj(x, g_mix[None, :], w, colgain, tm=1024)

    oa = _sb_attention(qkv, blk=256)
    ob = _moba_attention(qkv, tab_b)
    oc = _dilated_attention(d1, d4, d16, tabs_c)

    wb = w_branch.astype(BF16)
    ea, eb = H_A * HEAD_DIM, (H_A + H_B) * HEAD_DIM
    x = _merge_out(oa, ob, oc, gates, wb[:ea], wb[ea:eb], wb[eb:], w_out.astype(BF16), x, tm=256)
    return _ffn(x, g_ffn[None, :], w_gu.astype(BF16), w_down.astype(BF16), tm=512, tf=512)


def kernel(x, g_mix, w_in, q_gain, k_gain, w_branch, w_out, g_ffn, w_gu, w_down, rel_bias):
    depth = g_mix.shape[0]
    tab_b, tabs_c = _bias_tables(rel_bias, x.shape[1])
    for l in range(depth):
        x = _layer(x, g_mix[l], w_in[l], q_gain[l], k_gain[l], w_branch[l], w_out[l],
                   g_ffn[l], w_gu[l], w_down[l], tab_b, tabs_c)
    return x
```

```python
import functools
import math

import jax
import jax.numpy as jnp
from jax import lax
from jax.experimental import pallas as pl
from jax.experimental.pallas import tpu as pltpu

D_MODEL = 2048
HEAD_DIM = 128
H_A = 4
H_B = 6
H_C = 6
N_HEADS = H_A + H_B + H_C
MIX_WIDTH = N_HEADS * HEAD_DIM
N_BRANCH = 3
D_FF = 5632
MOBA_BLOCK = 256
MOBA_TOPK = 3
DILATIONS = ((128, 1), (512, 4), (2048, 16))
BAND_BLOCK = 128
N_BUCKETS = 32
MAX_DISTANCE = 2048
RMS_EPS = 1e-6
NEG_INF = -1e30

LANES = 128
VMEM_LIMIT = 56 * 1024 * 1024

SB_SKIP_SUM = 110.0

F32 = jnp.float32
BF16 = jnp.bfloat16

SLOTS_MAIN = 3 * (H_A + H_B)
SLOTS_DIL = 3 * H_C
SB_Q, SB_K, SB_V = 0, H_A, 2 * H_A
MB_Q, MB_K, MB_V = 3 * H_A, 3 * H_A + H_B, 3 * H_A + 2 * H_B


def _params(sem):
    return pltpu.CompilerParams(dimension_semantics=sem, vmem_limit_bytes=VMEM_LIMIT)


def _dot(a, b):
    return jnp.dot(a, b, preferred_element_type=F32)


def _dot_nt(a, b):
    return lax.dot_general(a, b, (((1,), (1,)), ((), ())), preferred_element_type=F32)


def _rows(n, size, length=None):
    start = n * size if isinstance(n, int) else pl.multiple_of(n * size, size)
    return pl.ds(start, size if length is None else length)


def _rms_rows(x):
    return x * lax.rsqrt(jnp.mean(x * x, axis=-1, keepdims=True) + RMS_EPS)


PROJ_HEADS = 6
PROJ_TN = PROJ_HEADS * LANES
J_MAIN = SLOTS_MAIN // PROJ_HEADS
J_DIL = SLOTS_DIL // PROJ_HEADS
J_GATE = N_BRANCH * D_MODEL // PROJ_TN
NORMED_TILES = (2, 3, J_MAIN, J_MAIN + 1)
PROJ_SUB = 256


def _proj_kernel(x_ref, g_ref, w_ref, cg_ref, main_ref, d1_ref, d4_ref, d16_ref, gates_ref, h_ref, y_ref):
    j = pl.program_id(2)
    tm = h_ref.shape[0]
    sub = PROJ_SUB

    @pl.when(j == 0)
    def _():
        h_ref[...] = (_rms_rows(x_ref[0]) * g_ref[...]).astype(BF16)

    is_gate = j >= J_MAIN + J_DIL
    is_dil = (j >= J_MAIN) & jnp.logical_not(is_gate)
    is_norm = functools.reduce(jnp.logical_or, [j == t for t in NORMED_TILES])

    def sweep(epilogue):
        for mi in range(tm // sub):
            rows = slice(mi * sub, (mi + 1) * sub)
            epilogue(mi, rows, _dot(h_ref[rows, :], w_ref[...]))

    def heads(acc, normed):
        for hh in range(PROJ_HEADS):
            sl = slice(hh * LANES, (hh + 1) * LANES)
            yield hh, (_rms_rows(acc[:, sl]) if normed else acc[:, sl]) * cg_ref[:, sl]

    @pl.when(is_gate)
    def _():
        def epilogue(mi, rows, acc):
            gates_ref[0, rows, :] = jax.nn.sigmoid(acc).astype(BF16)
        sweep(epilogue)

    for normed in (False, True):
        @pl.when((j < J_MAIN) & (is_norm == normed))
        def _():
            def epilogue(mi, rows, acc):
                for hh, y in heads(acc, normed):
                    main_ref[0, hh, rows, :] = y.astype(BF16)
            sweep(epilogue)

        @pl.when(is_dil & (is_norm == normed))
        def _():
            def epilogue(mi, rows, acc):
                for hh, y in heads(acc, normed):
                    d1_ref[0, hh, 0, rows, :] = y.astype(BF16)
                    y_ref[mi % 2, hh] = y
                    for r, out in ((4, d4_ref), (16, d16_ref)):
                        for c in range(r):
                            out[0, hh, c, mi * (sub // r):(mi + 1) * (sub // r), :] = (
                                y_ref[mi % 2, hh, pl.ds(c, sub // r, stride=r), :].astype(BF16))
            sweep(epilogue)


def _proj(x, g, w, colgain, *, tm):
    B, S, D = x.shape
    tn, hp = PROJ_TN, PROJ_HEADS
    jd, jg = J_MAIN, J_MAIN + J_DIL
    dil = lambda r: pl.BlockSpec((1, hp, r, tm // r, LANES),
                                 lambda b, s, j: (b, jnp.clip(j - jd, 0, J_DIL - 1), 0, s, 0))
    dil_shape = lambda r: jax.ShapeDtypeStruct((B, SLOTS_DIL, r, S // r, LANES), BF16)
    return pl.pallas_call(
        _proj_kernel,
        grid=(B, S // tm, J_MAIN + J_DIL + J_GATE),
        in_specs=[
            pl.BlockSpec((1, tm, D), lambda b, s, j: (b, s, 0)),
            pl.BlockSpec((1, D), lambda b, s, j: (0, 0)),
            pl.BlockSpec((D, tn), lambda b, s, j: (0, j)),
            pl.BlockSpec((1, tn), lambda b, s, j: (0, jnp.minimum(j, jg - 1))),
        ],
        out_specs=[
            pl.BlockSpec((1, hp, tm, LANES), lambda b, s, j: (b, jnp.minimum(j, jd - 1), s, 0)),
            dil(1), dil(4), dil(16),
            pl.BlockSpec((1, tm, tn), lambda b, s, j: (b, s, jnp.maximum(j - jg, 0))),
        ],
        out_shape=[
            jax.ShapeDtypeStruct((B, SLOTS_MAIN, S, LANES), BF16),
            dil_shape(1), dil_shape(4), dil_shape(16),
            jax.ShapeDtypeStruct((B, S, N_BRANCH * D_MODEL), BF16),
        ],
        scratch_shapes=[pltpu.VMEM((tm, D), BF16), pltpu.VMEM((2, hp, PROJ_SUB, LANES), F32)],
        compiler_params=_params(("arbitrary", "arbitrary", "arbitrary")),
        name="proj",
    )(x, g, w, colgain)


def _softplus(z):
    return jnp.maximum(z, 0.0) + jnp.log(1.0 + jnp.exp(-jnp.abs(z)))


def _sb_kernel(q_ref, k_ref, v_ref, u_ref, o_ref, *, blk):
    i = pl.program_id(1)
    u = u_ref[...]
    row = lax.broadcasted_iota(jnp.int32, (blk, blk), 0)
    col = lax.broadcasted_iota(jnp.int32, (blk, blk), 1)
    past = col < row

    def blocks(j, state, diag):
        keys = pl.ds(pl.multiple_of(j * blk, blk), blk)
        heads = range(H_A)
        z = [_dot_nt(q_ref[0, h], k_ref[0, h, keys, :]) for h in heads]
        sp = [_softplus(z[h]) for h in heads]
        spm = [jnp.where(past, sp[h], 0.0) for h in heads] if diag else sp
        hi = [spm[h].astype(BF16) for h in heads]
        lo = [(spm[h] - hi[h].astype(F32)).astype(BF16) for h in heads]
        after = [_dot(hi[h], u) + _dot(lo[h], u) for h in heads]
        w = [jnp.exp(z[h] - sp[h] - after[h] - state[h][0]) for h in heads]
        if diag:
            w = [jnp.where(past, w[h], 0.0) for h in heads]
        acc = [state[h][1] + _dot(w[h].astype(BF16), v_ref[0, h, keys, :]) for h in heads]
        c = [state[h][0] + after[h][:, :1] + spm[h][:, :1] for h in heads]
        return tuple(zip(c, acc))

    def smallest(state):
        return functools.reduce(jnp.minimum, [jnp.min(c) for c, _ in state])

    zeros = (jnp.zeros((blk, 1), F32), jnp.zeros((blk, HEAD_DIM), F32))
    state = blocks(i, (zeros,) * H_A, True)

    def cond(carry):
        j, c_min, _ = carry
        return (j >= 0) & (c_min < SB_SKIP_SUM)

    def body(carry):
        j, _, state = carry
        state = blocks(j, state, False)
        return j - 1, smallest(state), state

    _, _, state = lax.while_loop(cond, body, (i - 1, smallest(state), state))
    for h, (_, acc) in enumerate(state):
        o_ref[0, :, h * LANES:(h + 1) * LANES] = acc.astype(BF16)


def _sb_attention(qkv, *, blk):
    B, _, S, _ = qkv.shape
    tri = (jnp.arange(blk)[:, None] > jnp.arange(blk)[None, :]).astype(BF16)
    return pl.pallas_call(
        functools.partial(_sb_kernel, blk=blk),
        grid=(B, S // blk),
        in_specs=[
            pl.BlockSpec((1, H_A, blk, LANES), lambda b, i: (b, SB_Q // H_A, i, 0)),
            pl.BlockSpec((1, H_A, S, LANES), lambda b, i: (b, SB_K // H_A, 0, 0)),
            pl.BlockSpec((1, H_A, S, LANES), lambda b, i: (b, SB_V // H_A, 0, 0)),
            pl.BlockSpec((blk, blk), lambda b, i: (0, 0)),
        ],
        out_specs=pl.BlockSpec((1, blk, H_A * LANES), lambda b, i: (b, i, 0)),
        out_shape=jax.ShapeDtypeStruct((B, S, H_A * HEAD_DIM), BF16),
        compiler_params=_params(("arbitrary", "arbitrary")),
        name="sb_attention",
    )(qkv, qkv, qkv, tri)


def _toeplitz(row_vals, n_rows, shift):
    full = jnp.broadcast_to(row_vals, (n_rows, row_vals.shape[1]))
    return pltpu.roll(full, shift, 1, stride=1, stride_axis=0)


MOBA_STRIP = 64
GATE_UNROLL = 4


def _moba_kernel(q_ref, k_ref, v_ref, tab_ref, o_ref, tb_ref, ka_ref, va_ref, qa_ref, m_ref, accl_ref, sc_ref, p_ref,
                 al_ref, *, nblk, n_sel):
    blk = MOBA_BLOCK
    wide = 2 * blk
    npair = nblk // 2

    @pl.when(pl.program_id(1) == 0)
    def _():
        for d in range(nblk):
            tb_ref[d] = _toeplitz(tab_ref[0, d:d + 1, :], blk, blk + 1)[:, :blk]

    lane = lax.broadcasted_iota(jnp.int32, (blk, LANES), 1)
    va_ref[:, LANES:] = jnp.ones((va_ref.shape[0], LANES), BF16)
    means = []
    for n in range(nblk):
        rows = slice(n * blk, (n + 1) * blk)
        kb = k_ref[0, 0, rows, :]
        means.append(jnp.mean(kb.astype(F32), axis=0, keepdims=True))
        ka_ref[rows, :LANES] = kb
        ka_ref[rows, LANES:] = jnp.where(lane == n, 1.0, 0.0).astype(BF16)
        va_ref[rows, :LANES] = v_ref[0, 0, rows, :]
    km = jnp.concatenate(means, axis=0).astype(BF16)

    blk_id = lax.broadcasted_iota(jnp.int32, (nblk, blk), 0)
    row = lax.broadcasted_iota(jnp.int32, (blk, blk), 0)
    col = lax.broadcasted_iota(jnp.int32, (blk, blk), 1)
    eye = jnp.where(row == col, 1.0, 0.0).astype(BF16)
    causal = col <= row

    def gating(g, _):
        ids = [g * GATE_UNROLL + k for k in range(GATE_UNROLL)]
        windows = [pl.ds(pl.multiple_of(g * (GATE_UNROLL * blk), GATE_UNROLL * blk) + k * blk, blk)
                   for k in range(GATE_UNROLL)]
        q = [q_ref[0, 0, rows, :] for rows in windows]
        fully_past = [blk_id < i for i in ids]
        gate = [jnp.where(fp, _dot_nt(km, x), NEG_INF) for fp, x in zip(fully_past, q)]
        sel = [jnp.zeros((nblk, blk), F32) for _ in ids]
        for _ in range(n_sel):
            top = [jnp.max(x, axis=0, keepdims=True) for x in gate]
            first = [jnp.min(jnp.where(x == t, blk_id, nblk), axis=0, keepdims=True)
                     for x, t in zip(gate, top)]
            pick = [blk_id == f for f in first]
            sel = [jnp.where(pk, 1.0, x) for pk, x in zip(pick, sel)]
            gate = [jnp.where(pk, -jnp.inf, x) for pk, x in zip(pick, gate)]
        pad = jnp.zeros((LANES - nblk, blk), BF16)
        pen_t = [jnp.concatenate([jnp.where((blk_id == i) | (fp & (x > 0.0)), 0.0, NEG_INF).astype(BF16), pad], axis=0)
                 for i, fp, x in zip(ids, fully_past, sel)]
        pen = [_dot_nt(eye, x).astype(BF16) for x in pen_t]
        for rows, x, y in zip(windows, q, pen):
            qa_ref[rows, :LANES] = x
            qa_ref[rows, LANES:] = y
        return 0

    lax.fori_loop(0, nblk // GATE_UNROLL, gating, 0)


    def aligned(x, m):
        return x if isinstance(x, int) else pl.multiple_of(x, m)

    def soften(t, j, slot):
        s_base = aligned(j * wide, wide)
        q_base = s_base if t is None else aligned((t + 1 + j) * wide, wide)
        for par in range(2):
            for st in range(blk // MOBA_STRIP):
                tile = slice(st * MOBA_STRIP, (st + 1) * MOBA_STRIP)
                off = par * blk + st * MOBA_STRIP
                s = sc_ref[pl.ds(s_base + off, MOBA_STRIP), :]
                state = pl.ds(q_base + off, MOBA_STRIP)
                if t is None:
                    inside = jnp.where(causal[tile], s[:, par * blk:(par + 1) * blk] + tb_ref[0, tile, :], NEG_INF)
                    other = s[:, blk:] if par == 0 else s[:, :blk] + tb_ref[1, tile, :]
                    s0, s1 = (inside, other) if par == 0 else (other, inside)
                    m = jnp.max(jnp.maximum(s0, s1), axis=-1, keepdims=True)
                else:
                    d = 2 + 2 * j + par
                    s0 = s[:, :blk] + tb_ref[d, tile, :]
                    s1 = s[:, blk:] + tb_ref[d - 1, tile, :]
                    m_old = m_ref[state, :]
                    m = jnp.maximum(m_old, jnp.max(jnp.maximum(s0, s1), axis=-1, keepdims=True))
                    al_ref[slot, off:off + MOBA_STRIP, :] = jnp.exp(m_old - m)
                p_ref[slot, off:off + MOBA_STRIP, :blk] = jnp.exp(s0 - m).astype(BF16)
                p_ref[slot, off:off + MOBA_STRIP, blk:] = jnp.exp(s1 - m).astype(BF16)
                m_ref[state, :] = m

    def accumulate(t, j, slot):
        if t is None:
            rows = _rows(j, wide)
            accl_ref[rows, :] = _dot(p_ref[slot], va_ref[rows, :])
        else:
            rows = _rows(t + 1 + j, wide)
            accl_ref[rows, :] = al_ref[slot] * accl_ref[rows, :] + _dot(p_ref[slot], va_ref[t * wide:(t + 1) * wide, :])

    def phase(t, n):
        soften(t, 0, 0)

        def body(j, _):
            accumulate(t, j - 1, (j - 1) % 2)
            soften(t, j, j % 2)
            return 0

        lax.fori_loop(1, n, body, 0)
        accumulate(t, n - 1, (n - 1) % 2)

    for u in range(npair):
        rows = slice(u * wide, (u + 1) * wide)
        sc_ref[rows, :] = _dot_nt(qa_ref[rows, :], ka_ref[rows, :])
    phase(None, npair)
    for t in range(npair - 1):
        n = npair - 1 - t
        sc_ref[0:n * wide, :] = _dot_nt(qa_ref[(t + 1) * wide:, :], ka_ref[t * wide:(t + 1) * wide, :])
        phase(t, n)


    def finish(i, _):
        rows = _rows(i, blk)
        o_ref[0, rows, :] = (accl_ref[rows, :LANES] / accl_ref[rows, LANES:]).astype(BF16)
        return 0

    lax.fori_loop(0, nblk, finish, 0)


def _moba_attention(qkv, tab):
    B, _, S, _ = qkv.shape
    blk = MOBA_BLOCK
    nblk = S // blk
    assert nblk % 2 == 0 and nblk % GATE_UNROLL == 0 and nblk <= LANES
    n_sel = min(MOBA_TOPK, nblk - 1)
    head = lambda slot: pl.BlockSpec((1, 1, S, LANES), lambda h, b: (b, slot + h, 0, 0))
    return pl.pallas_call(
        functools.partial(_moba_kernel, nblk=nblk, n_sel=n_sel),
        grid=(H_B, B),
        in_specs=[head(MB_Q), head(MB_K), head(MB_V), pl.BlockSpec((1, nblk, 2 * blk), lambda h, b: (h, 0, 0))],
        out_specs=pl.BlockSpec((1, S, LANES), lambda h, b: (b, 0, h)),
        out_shape=jax.ShapeDtypeStruct((B, S, H_B * HEAD_DIM), BF16),
        scratch_shapes=[pltpu.VMEM((nblk, blk, blk), F32),
                        pltpu.VMEM((S, 2 * LANES), BF16), pltpu.VMEM((S, 2 * LANES), BF16),
                        pltpu.VMEM((S, 2 * LANES), BF16), pltpu.VMEM((S, 1), F32),
                        pltpu.VMEM((S, 2 * LANES), F32), pltpu.VMEM((S, 2 * blk), F32),
                        pltpu.VMEM((2, 2 * blk, 2 * blk), BF16), pltpu.VMEM((2, 2 * blk, 1), F32)],
        compiler_params=_params(("arbitrary", "arbitrary")),
        name="moba_attention",
    )(qkv, qkv, qkv, tab)


DIL_UNROLL = 8


def _dilated_kernel(q1_ref, k1_ref, v1_ref, q4_ref, k4_ref, v4_ref, q16_ref, k16_ref, v16_ref, tab_ref, o_ref,
                    c4o_ref, c4l_ref, c16o_ref, c16l_ref, s4o_ref, s4l_ref, s16o_ref, s16l_ref, *, seq):
    bb = BAND_BLOCK
    qi = lax.broadcasted_iota(jnp.int32, (bb, 2 * bb), 0)
    kj = lax.broadcasted_iota(jnp.int32, (bb, 2 * bb), 1)
    visible = (kj >= qi) & (kj <= qi + bb)

    def pattern(g, r, q_ref, k_ref, v_ref, emit):
        bias = jnp.where(visible, _toeplitz(tab_ref[0, g:g + 1, :], bb, 0), NEG_INF)
        bias_first = jnp.concatenate([bias[:, bb:], jnp.full((bb, bb), NEG_INF, F32)], axis=1)
        nb = seq // r // bb

        def body(t, _):
            blocks = [((t * DIL_UNROLL + k) // nb, (t * DIL_UNROLL + k) % nb) for k in range(DIL_UNROLL)]
            keys = [_rows(jnp.maximum(n - 1, 0), bb, 2 * bb) for _, n in blocks]
            s = [_dot_nt(q_ref[0, 0, c, _rows(n, bb), :], k_ref[0, 0, c, kw, :]) + jnp.where(n == 0, bias_first, bias)
                 for (c, n), kw in zip(blocks, keys)]
            m = [jnp.max(x, axis=-1, keepdims=True) for x in s]
            p = [jnp.exp(x - mx) for x, mx in zip(s, m)]
            den = [jnp.sum(x, axis=-1, keepdims=True) for x in p]
            o = [_dot(x.astype(BF16), v_ref[0, 0, c, kw, :]) for x, (c, _), kw in zip(p, blocks, keys)]
            for (c, n), ox, dx, mx in zip(blocks, o, den, m):
                emit(c, n, ox / dx, mx + jnp.log(dx))
            return 0

        lax.fori_loop(0, r * nb // DIL_UNROLL, body, 0)

    def class_major(o_cm, l_cm):
        def emit(c, n, o, lse):
            o_cm[c, _rows(n, bb), :] = o
            l_cm[c, _rows(n, bb), :] = jnp.broadcast_to(lse, (bb, LANES))
        return emit

    def to_sequence(r, cm, sq):
        for c in range(r):
            sq[pl.ds(c, seq // r, stride=r), :] = cm[c]

    pattern(2, 16, q16_ref, k16_ref, v16_ref, class_major(c16o_ref, c16l_ref))
    to_sequence(16, c16o_ref, s16o_ref)
    to_sequence(16, c16l_ref, s16l_ref)
    pattern(1, 4, q4_ref, k4_ref, v4_ref, class_major(c4o_ref, c4l_ref))
    to_sequence(4, c4o_ref, s4o_ref)
    to_sequence(4, c4l_ref, s4l_ref)

    def merge(c, n, o1, l1):
        rows = _rows(n, bb)
        l4, l16 = s4l_ref[rows, :], s16l_ref[rows, :]
        top = jnp.maximum(jnp.maximum(l1, l4), l16)
        e1, e4, e16 = jnp.exp(l1 - top), jnp.exp(l4 - top), jnp.exp(l16 - top)
        mix = e1 * o1 + e4 * s4o_ref[rows, :] + e16 * s16o_ref[rows, :]
        o_ref[0, rows, :] = (mix / (e1 + e4 + e16)).astype(BF16)

    pattern(0, 1, q1_ref, k1_ref, v1_ref, merge)


def _dilated_attention(d1, d4, d16, tabs):
    B, S = d1.shape[0], d1.shape[3]
    assert [r for _, r in DILATIONS] == [1, 4, 16] and all(w // r == BAND_BLOCK for w, r in DILATIONS)
    assert (S // BAND_BLOCK) % DIL_UNROLL == 0
    head = lambda r, part: pl.BlockSpec((1, 1, r, S // r, LANES), lambda b, h: (b, part * H_C + h, 0, 0, 0))
    class_major = lambda r: pltpu.VMEM((r, S // r, LANES), F32)
    return pl.pallas_call(
        functools.partial(_dilated_kernel, seq=S),
        grid=(B, H_C),
        in_specs=[head(r, part) for r in (1, 4, 16) for part in range(3)]
        + [pl.BlockSpec((1, len(DILATIONS), 2 * BAND_BLOCK), lambda b, h: (h, 0, 0))],
        out_specs=pl.BlockSpec((1, S, LANES), lambda b, h: (b, 0, h)),
        out_shape=jax.ShapeDtypeStruct((B, S, H_C * HEAD_DIM), BF16),
        scratch_shapes=[class_major(4), class_major(4), class_major(16), class_major(16)]
        + [pltpu.VMEM((S, LANES), F32)] * 4,
        compiler_params=_params(("arbitrary", "arbitrary")),
        name="dilated_attention",
    )(d1, d1, d1, d4, d4, d4, d16, d16, d16, tabs)


def _merge_kernel(oa_ref, ob_ref, oc_ref, g0_ref, g1_ref, g2_ref, wbr_ref, wo_ref, x_ref, out_ref):
    ea, eb = H_A * HEAD_DIM, (H_A + H_B) * HEAD_DIM
    merged = (g0_ref[0].astype(F32) * _dot(oa_ref[0], wbr_ref[:ea, :])
              + g1_ref[0].astype(F32) * _dot(ob_ref[0], wbr_ref[ea:eb, :])
              + g2_ref[0].astype(F32) * _dot(oc_ref[0], wbr_ref[eb:, :]))
    out_ref[0] = x_ref[0] + _dot(merged.astype(BF16), wo_ref[...])


def _merge_out(oa, ob, oc, gates, w_branch, wo, x, *, tm):
    B, S, D = x.shape
    rows = lambda w: pl.BlockSpec((1, tm, w), lambda b, s: (b, s, 0))
    gate = lambda g: pl.BlockSpec((1, tm, D), lambda b, s: (b, s, g))
    resident = lambda w: pl.BlockSpec(w.shape, lambda b, s: (0, 0), pipeline_mode=pl.Buffered(1))
    return pl.pallas_call(
        _merge_kernel,
        grid=(B, S // tm),
        in_specs=[rows(oa.shape[-1]), rows(ob.shape[-1]), rows(oc.shape[-1]), gate(0), gate(1), gate(2),
                  resident(w_branch), resident(wo), rows(D)],
        out_specs=rows(D),
        out_shape=jax.ShapeDtypeStruct((B, S, D), F32),
        compiler_params=_params(("arbitrary", "arbitrary")),
        name="merge_out",
    )(oa, ob, oc, gates, gates, gates, w_branch, wo, x)


FFN_SUB = 256


def _ffn_kernel(x_ref, g_ref, wg_ref, wu_ref, wd_ref, o_ref, h_ref):
    @pl.when(pl.program_id(2) == 0)
    def _():
        x = x_ref[0]
        h_ref[...] = (_rms_rows(x) * g_ref[...]).astype(BF16)
        o_ref[0] = x

    for mi in range(h_ref.shape[0] // FFN_SUB):
        rows = slice(mi * FFN_SUB, (mi + 1) * FFN_SUB)
        h = h_ref[rows, :]
        gate = _dot(h, wg_ref[...])
        up = _dot(h, wu_ref[...])
        act = (gate * jax.nn.sigmoid(gate) * up).astype(BF16)
        o_ref[0, rows, :] += _dot(act, wd_ref[...])


def _ffn(x, g, w_gu, w_down, *, tm, tf):
    B, S, D = x.shape
    nf = D_FF // tf
    return pl.pallas_call(
        _ffn_kernel,
        grid=(B, S // tm, nf),
        in_specs=[
            pl.BlockSpec((1, tm, D), lambda b, s, f: (b, s, 0)),
            pl.BlockSpec((1, D), lambda b, s, f: (0, 0)),
            pl.BlockSpec((D, tf), lambda b, s, f: (0, f)),
            pl.BlockSpec((D, tf), lambda b, s, f: (0, nf + f)),
            pl.BlockSpec((tf, D), lambda b, s, f: (f, 0)),
        ],
        out_specs=pl.BlockSpec((1, tm, D), lambda b, s, f: (b, s, 0)),
        out_shape=jax.ShapeDtypeStruct((B, S, D), F32),
        scratch_shapes=[pltpu.VMEM((tm, D), BF16)],
        compiler_params=_params(("arbitrary", "arbitrary", "arbitrary")),
        name="ffn",
    )(x, g, w_gu, w_gu, w_down)


def _t5_bucket(dist):
    max_exact = N_BUCKETS // 2
    d = jnp.maximum(dist, 0)
    df = jnp.maximum(d, 1).astype(F32)
    large = max_exact + (jnp.log(df / max_exact) / math.log(MAX_DISTANCE / max_exact)
                         * (N_BUCKETS - max_exact)).astype(jnp.int32)
    large = jnp.minimum(large, N_BUCKETS - 1)
    return jnp.where(d < max_exact, d, large)


def _bias_tables(rel_bias, S):
    by_dist = rel_bias[_t5_bucket(jnp.arange(S))].T.astype(F32)
    nblk, blk, bb = S // MOBA_BLOCK, MOBA_BLOCK, BAND_BLOCK
    rev = jnp.pad(by_dist[:H_B], ((0, 0), (blk, 0)))[:, ::-1]
    tab_b = jnp.stack([rev[:, S - blk * (d + 1): S - blk * (d - 1)] for d in range(nblk)], axis=1)
    tabs_c = []
    for _, r in DILATIONS:
        near = by_dist[H_B:, ::r][:, :bb + 1]
        tabs_c.append(jnp.pad(near[:, ::-1], ((0, 0), (0, bb - 1))))
    return tab_b, jnp.stack(tabs_c, axis=1)


def _proj_weights(w_in, q_gain, k_gain):
    scale = HEAD_DIM ** -0.5
    ones = lambda n: jnp.ones((n * HEAD_DIM,), F32)
    hb = H_A + H_B

    def cols(part, h0, h1):
        return w_in[:, part * MIX_WIDTH + h0 * HEAD_DIM: part * MIX_WIDTH + h1 * HEAD_DIM]

    qg, kg = q_gain.astype(F32) * scale, k_gain.astype(F32)
    w = jnp.concatenate(
        [cols(p, 0, H_A) for p in range(3)] + [cols(p, H_A, hb) for p in range(3)]
        + [cols(p, hb, N_HEADS) for p in range(3)] + [w_in[:, 3 * MIX_WIDTH:]], axis=1).astype(BF16)
    colgain = jnp.concatenate([
        ones(H_A) * scale, ones(H_A), ones(H_A),
        qg[:H_B].reshape(-1), kg[:H_B].reshape(-1), ones(H_B),
        qg[H_B:].reshape(-1), kg[H_B:].reshape(-1), ones(H_C)])[None, :]
    return w, colgain


def _layer(x, g_mix, w_in, q_gain, k_gain, w_branch, w_out, g_ffn, w_gu, w_down, tab_b, tabs_c):
    w, colgain = _proj_weights(w_in, q_gain, k_gain)
    qkv, d1, d4, d16, gates = _proj(x, g_mix[None, :], w, colgain, tm=1024)

    oa = _sb_attention(qkv, blk=256)
    ob = _moba_attention(qkv, tab_b)
    oc = _dilated_attention(d1, d4, d16, tabs_c)

    x = _merge_out(oa, ob, oc, gates, w_branch.astype(BF16), w_out.astype(BF16), x, tm=256)
    return _ffn(x, g_ffn[None, :], w_gu.astype(BF16), w_down.astype(BF16), tm=1024, tf=512)


def kernel(x, g_mix, w_in, q_gain, k_gain, w_branch, w_out, g_ffn, w_gu, w_down, rel_bias):
    depth = g_mix.shape[0]
    tab_b, tabs_c = _bias_tables(rel_bias, x.shape[1])
    for l in range(depth):
        x = _layer(x, g_mix[l], w_in[l], q_gain[l], k_gain[l], w_branch[l], w_out[l],
                   g_ffn[l], w_gu[l], w_down[l], tab_b, tabs_c)
    return x
```

```python
import functools
import math

import jax
import jax.numpy as jnp
from jax import lax
from jax.experimental import pallas as pl
from jax.experimental.pallas import tpu as pltpu

D_MODEL = 2048
HEAD_DIM = 128
H_A = 4
H_B = 6
H_C = 6
N_HEADS = H_A + H_B + H_C
MIX_WIDTH = N_HEADS * HEAD_DIM
N_BRANCH = 3
D_FF = 5632
MOBA_BLOCK = 256
MOBA_TOPK = 3
DILATIONS = ((128, 1), (512, 4), (2048, 16))
BAND_BLOCK = 128
N_BUCKETS = 32
MAX_DISTANCE = 2048
RMS_EPS = 1e-6
NEG_INF = -1e30
LOG2E = math.log2(math.e)

LANES = 128
VMEM_LIMIT = 56 * 1024 * 1024

SB_SKIP_SUM = 110.0

F32 = jnp.float32
BF16 = jnp.bfloat16

SLOTS_MAIN = 3 * (H_A + H_B)
SLOTS_DIL = 3 * H_C
SB_Q, SB_K, SB_V = 0, H_A, 2 * H_A
MB_Q, MB_K, MB_V = 3 * H_A, 3 * H_A + H_B, 3 * H_A + 2 * H_B


def _params(sem):
    return pltpu.CompilerParams(dimension_semantics=sem, vmem_limit_bytes=VMEM_LIMIT)


def _dot(a, b):
    return jnp.dot(a, b, preferred_element_type=F32)


def _dot_nt(a, b):
    return lax.dot_general(a, b, (((1,), (1,)), ((), ())), preferred_element_type=F32)


def _rows(n, size, length=None):
    start = n * size if isinstance(n, int) else pl.multiple_of(n * size, size)
    return pl.ds(start, size if length is None else length)


def _rms_rows(x):
    return x * lax.rsqrt(jnp.mean(x * x, axis=-1, keepdims=True) + RMS_EPS)


PROJ_HEADS = 6
PROJ_TN = PROJ_HEADS * LANES
J_MAIN = SLOTS_MAIN // PROJ_HEADS
J_DIL = SLOTS_DIL // PROJ_HEADS
J_GATE = N_BRANCH * D_MODEL // PROJ_TN
NORMED_TILES = (2, 3, J_MAIN, J_MAIN + 1)
PROJ_SUB = 256


def _proj_kernel(x_ref, g_ref, w_ref, cg_ref, main_ref, d1_ref, d4_ref, d16_ref, gates_ref, h_ref, y_ref):
    j = pl.program_id(2)
    tm = h_ref.shape[0]
    sub = PROJ_SUB

    @pl.when(j == 0)
    def _():
        h_ref[...] = (_rms_rows(x_ref[0]) * g_ref[...]).astype(BF16)

    is_gate = j >= J_MAIN + J_DIL
    is_dil = (j >= J_MAIN) & jnp.logical_not(is_gate)
    is_norm = functools.reduce(jnp.logical_or, [j == t for t in NORMED_TILES])

    def sweep(epilogue):
        for mi in range(tm // sub):
            rows = slice(mi * sub, (mi + 1) * sub)
            epilogue(mi, rows, _dot(h_ref[rows, :], w_ref[...]))

    def heads(acc, normed):
        for hh in range(PROJ_HEADS):
            sl = slice(hh * LANES, (hh + 1) * LANES)
            yield hh, (_rms_rows(acc[:, sl]) if normed else acc[:, sl]) * cg_ref[:, sl]

    @pl.when(is_gate)
    def _():
        def epilogue(mi, rows, acc):
            gates_ref[0, rows, :] = jax.nn.sigmoid(acc).astype(BF16)
        sweep(epilogue)

    for normed in (False, True):
        @pl.when((j < J_MAIN) & (is_norm == normed))
        def _():
            def epilogue(mi, rows, acc):
                for hh, y in heads(acc, normed):
                    main_ref[0, hh, rows, :] = y.astype(BF16)
            sweep(epilogue)

        @pl.when(is_dil & (is_norm == normed))
        def _():
            def epilogue(mi, rows, acc):
                for hh, y in heads(acc, normed):
                    d1_ref[0, hh, 0, rows, :] = y.astype(BF16)
                    y_ref[mi % 2, hh] = y
                    for r, out in ((4, d4_ref), (16, d16_ref)):
                        for c in range(r):
                            out[0, hh, c, mi * (sub // r):(mi + 1) * (sub // r), :] = (
                                y_ref[mi % 2, hh, pl.ds(c, sub // r, stride=r), :].astype(BF16))
            sweep(epilogue)


def _proj(x, g, w, colgain, *, tm):
    B, S, D = x.shape
    tn, hp = PROJ_TN, PROJ_HEADS
    jd, jg = J_MAIN, J_MAIN + J_DIL
    dil = lambda r: pl.BlockSpec((1, hp, r, tm // r, LANES),
                                 lambda b, s, j: (b, jnp.clip(j - jd, 0, J_DIL - 1), 0, s, 0))
    dil_shape = lambda r: jax.ShapeDtypeStruct((B, SLOTS_DIL, r, S // r, LANES), BF16)
    return pl.pallas_call(
        _proj_kernel,
        grid=(B, S // tm, J_MAIN + J_DIL + J_GATE),
        in_specs=[
            pl.BlockSpec((1, tm, D), lambda b, s, j: (b, s, 0)),
            pl.BlockSpec((1, D), lambda b, s, j: (0, 0)),
            pl.BlockSpec((D, tn), lambda b, s, j: (0, j)),
            pl.BlockSpec((1, tn), lambda b, s, j: (0, jnp.minimum(j, jg - 1))),
        ],
        out_specs=[
            pl.BlockSpec((1, hp, tm, LANES), lambda b, s, j: (b, jnp.minimum(j, jd - 1), s, 0)),
            dil(1), dil(4), dil(16),
            pl.BlockSpec((1, tm, tn), lambda b, s, j: (b, s, jnp.maximum(j - jg, 0))),
        ],
        out_shape=[
            jax.ShapeDtypeStruct((B, SLOTS_MAIN, S, LANES), BF16),
            dil_shape(1), dil_shape(4), dil_shape(16),
            jax.ShapeDtypeStruct((B, S, N_BRANCH * D_MODEL), BF16),
        ],
        scratch_shapes=[pltpu.VMEM((tm, D), BF16), pltpu.VMEM((2, hp, PROJ_SUB, LANES), F32)],
        compiler_params=_params(("arbitrary", "arbitrary", "arbitrary")),
        name="proj",
    )(x, g, w, colgain)


def _softplus(z):
    return jnp.maximum(z, 0.0) + jnp.log(1.0 + jnp.exp(-jnp.abs(z)))


def _sb_kernel(q_ref, k_ref, v_ref, u_ref, o_ref, *, blk):
    i = pl.program_id(1)
    u = u_ref[...]
    row = lax.broadcasted_iota(jnp.int32, (blk, blk), 0)
    col = lax.broadcasted_iota(jnp.int32, (blk, blk), 1)
    past = col < row

    def blocks(j, state, diag):
        keys = pl.ds(pl.multiple_of(j * blk, blk), blk)
        heads = range(H_A)
        z = [_dot_nt(q_ref[0, h], k_ref[0, h, keys, :]) for h in heads]
        sp = [_softplus(z[h]) for h in heads]
        spm = [jnp.where(past, sp[h], 0.0) for h in heads] if diag else sp
        hi = [spm[h].astype(BF16) for h in heads]
        lo = [(spm[h] - hi[h].astype(F32)).astype(BF16) for h in heads]
        after = [_dot(hi[h], u) + _dot(lo[h], u) for h in heads]
        w = [jnp.exp(z[h] - sp[h] - after[h] - state[h][0]) for h in heads]
        if diag:
            w = [jnp.where(past, w[h], 0.0) for h in heads]
        acc = [state[h][1] + _dot(w[h].astype(BF16), v_ref[0, h, keys, :]) for h in heads]
        c = [state[h][0] + after[h][:, :1] + spm[h][:, :1] for h in heads]
        return tuple(zip(c, acc))

    def smallest(state):
        return functools.reduce(jnp.minimum, [jnp.min(c) for c, _ in state])

    zeros = (jnp.zeros((blk, 1), F32), jnp.zeros((blk, HEAD_DIM), F32))
    state = blocks(i, (zeros,) * H_A, True)

    def cond(carry):
        j, c_min, _ = carry
        return (j >= 0) & (c_min < SB_SKIP_SUM)

    def body(carry):
        j, _, state = carry
        state = blocks(j, state, False)
        return j - 1, smallest(state), state

    _, _, state = lax.while_loop(cond, body, (i - 1, smallest(state), state))
    for h, (_, acc) in enumerate(state):
        o_ref[0, :, h * LANES:(h + 1) * LANES] = acc.astype(BF16)


def _sb_attention(qkv, *, blk):
    B, _, S, _ = qkv.shape
    tri = (jnp.arange(blk)[:, None] > jnp.arange(blk)[None, :]).astype(BF16)
    return pl.pallas_call(
        functools.partial(_sb_kernel, blk=blk),
        grid=(B, S // blk),
        in_specs=[
            pl.BlockSpec((1, H_A, blk, LANES), lambda b, i: (b, SB_Q // H_A, i, 0)),
            pl.BlockSpec((1, H_A, S, LANES), lambda b, i: (b, SB_K // H_A, 0, 0)),
            pl.BlockSpec((1, H_A, S, LANES), lambda b, i: (b, SB_V // H_A, 0, 0)),
            pl.BlockSpec((blk, blk), lambda b, i: (0, 0)),
        ],
        out_specs=pl.BlockSpec((1, blk, H_A * LANES), lambda b, i: (b, i, 0)),
        out_shape=jax.ShapeDtypeStruct((B, S, H_A * HEAD_DIM), BF16),
        compiler_params=_params(("arbitrary", "arbitrary")),
        name="sb_attention",
    )(qkv, qkv, qkv, tri)


def _toeplitz(row_vals, n_rows, shift):
    full = jnp.broadcast_to(row_vals, (n_rows, row_vals.shape[1]))
    return pltpu.roll(full, shift, 1, stride=1, stride_axis=0)


MOBA_STRIP = 32
GATE_UNROLL = 4


def _moba_kernel(q_ref, k_ref, v_ref, tab_ref, o_ref, tb_ref, ob_ref, ka_ref, va_ref, qa_ref, m_ref, accl_ref, sc_ref,
                 p_ref, al_ref, *, nblk, n_sel):
    blk = MOBA_BLOCK
    wide = 2 * blk
    npair = nblk // 2

    row = lax.broadcasted_iota(jnp.int32, (blk, blk), 0)
    col = lax.broadcasted_iota(jnp.int32, (blk, blk), 1)

    @pl.when(pl.program_id(1) == 0)
    def _():
        for d in range(nblk):
            tb_ref[d] = _toeplitz(tab_ref[0, d:d + 1, :], blk, blk + 1)[:, :blk]
        diag = jnp.where(col <= row, tb_ref[0], NEG_INF)
        ob_ref[:blk, :blk] = diag
        ob_ref[:blk, blk:] = jnp.zeros((blk, blk), F32)
        ob_ref[blk:, :blk] = tb_ref[1]
        ob_ref[blk:, blk:] = diag

    lane = lax.broadcasted_iota(jnp.int32, (blk, LANES), 1)
    va_ref[:, LANES:] = jnp.ones((va_ref.shape[0], LANES), BF16)
    means = []
    for n in range(nblk):
        rows = slice(n * blk, (n + 1) * blk)
        kb = k_ref[0, 0, rows, :]
        means.append(jnp.mean(kb.astype(F32), axis=0, keepdims=True))
        ka_ref[rows, :LANES] = kb
        ka_ref[rows, LANES:] = jnp.where(lane == n, 1.0, 0.0).astype(BF16)
        va_ref[rows, :LANES] = v_ref[0, 0, rows, :]
    km = jnp.concatenate(means, axis=0).astype(BF16)

    blk_id = lax.broadcasted_iota(jnp.int32, (nblk, blk), 0)
    eye = jnp.where(row == col, 1.0, 0.0).astype(BF16)

    def gating(g, _):
        ids = [g * GATE_UNROLL + k for k in range(GATE_UNROLL)]
        windows = [pl.ds(pl.multiple_of(g * (GATE_UNROLL * blk), GATE_UNROLL * blk) + k * blk, blk)
                   for k in range(GATE_UNROLL)]
        q = [q_ref[0, 0, rows, :] for rows in windows]
        fully_past = [blk_id < i for i in ids]
        gate = [jnp.where(fp, _dot_nt(km, x), NEG_INF) for fp, x in zip(fully_past, q)]
        sel = [jnp.zeros((nblk, blk), F32) for _ in ids]
        for _ in range(n_sel):
            top = [jnp.max(x, axis=0, keepdims=True) for x in gate]
            first = [jnp.min(jnp.where(x == t, blk_id, nblk), axis=0, keepdims=True)
                     for x, t in zip(gate, top)]
            pick = [blk_id == f for f in first]
            sel = [jnp.where(pk, 1.0, x) for pk, x in zip(pick, sel)]
            gate = [jnp.where(pk, -jnp.inf, x) for pk, x in zip(pick, gate)]
        pad = jnp.zeros((LANES - nblk, blk), BF16)
        pen_t = [jnp.concatenate([jnp.where((blk_id == i) | (fp & (x > 0.0)), 0.0, NEG_INF).astype(BF16), pad], axis=0)
                 for i, fp, x in zip(ids, fully_past, sel)]
        pen = [_dot_nt(eye, x).astype(BF16) for x in pen_t]
        for rows, x, y in zip(windows, q, pen):
            qa_ref[rows, :LANES] = x
            qa_ref[rows, LANES:] = y
        return 0

    lax.fori_loop(0, nblk // GATE_UNROLL, gating, 0)


    def aligned(x, m):
        return x if isinstance(x, int) else pl.multiple_of(x, m)

    def pair_bias(d):
        return jnp.concatenate([jnp.concatenate([tb_ref[d], tb_ref[d - 1]], axis=1),
                                jnp.concatenate([tb_ref[d + 1], tb_ref[d]], axis=1)], axis=0)

    def soften(t, j, slot):
        s_base = aligned(j * wide, wide)
        q_base = s_base if t is None else aligned((t + 1 + j) * wide, wide)
        for st in range(wide // MOBA_STRIP):
            off = st * MOBA_STRIP
            s = sc_ref[pl.ds(s_base + off, MOBA_STRIP), :]
            state = pl.ds(q_base + off, MOBA_STRIP)
            m = jnp.broadcast_to(jnp.max(s, axis=-1, keepdims=True), (MOBA_STRIP, LANES))
            if t is not None:
                m_old = m_ref[state, :]
                m = jnp.maximum(m_old, m)
                al_ref[slot, off:off + MOBA_STRIP, :] = jnp.exp2(m_old - m)
            p_ref[slot, off:off + MOBA_STRIP, :] = jnp.exp2(s - pltpu.repeat(m, wide // LANES, axis=1)).astype(BF16)
            m_ref[state, :] = m

    def accumulate(t, j, slot):
        if t is None:
            rows = _rows(j, wide)
            accl_ref[rows, :] = _dot(p_ref[slot], va_ref[rows, :])
        else:
            rows = _rows(t + 1 + j, wide)
            accl_ref[rows, :] = (pltpu.repeat(al_ref[slot], 2, axis=1) * accl_ref[rows, :]
                                 + _dot(p_ref[slot], va_ref[t * wide:(t + 1) * wide, :]))

    def phase(t, n):
        soften(t, 0, 0)

        def body(j, _):
            accumulate(t, j - 1, (j - 1) % 2)
            soften(t, j, j % 2)
            return 0

        lax.fori_loop(1, n, body, 0)
        accumulate(t, n - 1, (n - 1) % 2)

    for u in range(npair):
        rows = slice(u * wide, (u + 1) * wide)
        sc_ref[rows, :] = _dot_nt(qa_ref[rows, :], ka_ref[rows, :]) + ob_ref[...]
    phase(None, npair)
    for t in range(npair - 1):
        n = npair - 1 - t
        for j in range(n):
            rows = slice((t + 1 + j) * wide, (t + 2 + j) * wide)
            sc_ref[j * wide:(j + 1) * wide, :] = (_dot_nt(qa_ref[rows, :], ka_ref[t * wide:(t + 1) * wide, :])
                                                  + pair_bias(2 + 2 * j))
        phase(t, n)


    def finish(i, _):
        rows = _rows(i, blk)
        o_ref[0, rows, :] = (accl_ref[rows, :LANES] / accl_ref[rows, LANES:]).astype(BF16)
        return 0

    lax.fori_loop(0, nblk, finish, 0)


def _moba_attention(qkv, tab):
    B, _, S, _ = qkv.shape
    blk = MOBA_BLOCK
    nblk = S // blk
    assert nblk % 2 == 0 and nblk % GATE_UNROLL == 0 and nblk <= LANES
    n_sel = min(MOBA_TOPK, nblk - 1)
    head = lambda slot: pl.BlockSpec((1, 1, S, LANES), lambda h, b: (b, slot + h, 0, 0))
    return pl.pallas_call(
        functools.partial(_moba_kernel, nblk=nblk, n_sel=n_sel),
        grid=(H_B, B),
        in_specs=[head(MB_Q), head(MB_K), head(MB_V), pl.BlockSpec((1, nblk, 2 * blk), lambda h, b: (h, 0, 0))],
        out_specs=pl.BlockSpec((1, S, LANES), lambda h, b: (b, 0, h)),
        out_shape=jax.ShapeDtypeStruct((B, S, H_B * HEAD_DIM), BF16),
        scratch_shapes=[pltpu.VMEM((nblk, blk, blk), F32), pltpu.VMEM((2 * blk, 2 * blk), F32),
                        pltpu.VMEM((S, 2 * LANES), BF16), pltpu.VMEM((S, 2 * LANES), BF16),
                        pltpu.VMEM((S, 2 * LANES), BF16), pltpu.VMEM((S, LANES), F32),
                        pltpu.VMEM((S, 2 * LANES), F32), pltpu.VMEM((S, 2 * blk), F32),
                        pltpu.VMEM((2, 2 * blk, 2 * blk), BF16), pltpu.VMEM((2, 2 * blk, LANES), F32)],
        compiler_params=_params(("arbitrary", "arbitrary")),
        name="moba_attention",
    )(qkv, qkv, qkv, tab)


DIL_UNROLL = 8


def _dilated_kernel(q1_ref, k1_ref, v1_ref, q4_ref, k4_ref, v4_ref, q16_ref, k16_ref, v16_ref, tab_ref, o_ref,
                    c4o_ref, c4l_ref, c16o_ref, c16l_ref, s4o_ref, s4l_ref, s16o_ref, s16l_ref, *, seq):
    bb = BAND_BLOCK
    qi = lax.broadcasted_iota(jnp.int32, (bb, 2 * bb), 0)
    kj = lax.broadcasted_iota(jnp.int32, (bb, 2 * bb), 1)
    visible = (kj >= qi) & (kj <= qi + bb)

    def pattern(g, r, q_ref, k_ref, v_ref, emit):
        bias = jnp.where(visible, _toeplitz(tab_ref[0, g:g + 1, :], bb, 0), NEG_INF)
        bias_first = jnp.concatenate([bias[:, bb:], jnp.full((bb, bb), NEG_INF, F32)], axis=1)
        nb = seq // r // bb

        def body(t, _):
            blocks = [((t * DIL_UNROLL + k) // nb, (t * DIL_UNROLL + k) % nb) for k in range(DIL_UNROLL)]
            keys = [_rows(jnp.maximum(n - 1, 0), bb, 2 * bb) for _, n in blocks]
            s = [_dot_nt(q_ref[0, 0, c, _rows(n, bb), :], k_ref[0, 0, c, kw, :]) + jnp.where(n == 0, bias_first, bias)
                 for (c, n), kw in zip(blocks, keys)]
            m = [jnp.max(x, axis=-1, keepdims=True) for x in s]
            p = [jnp.exp2(x - mx) for x, mx in zip(s, m)]
            den = [jnp.sum(x, axis=-1, keepdims=True) for x in p]
            o = [_dot(x.astype(BF16), v_ref[0, 0, c, kw, :]) for x, (c, _), kw in zip(p, blocks, keys)]
            for (c, n), ox, dx, mx in zip(blocks, o, den, m):
                emit(c, n, ox / dx, mx + jnp.log2(dx))
            return 0

        lax.fori_loop(0, r * nb // DIL_UNROLL, body, 0)

    def class_major(o_cm, l_cm):
        def emit(c, n, o, lse):
            o_cm[c, _rows(n, bb), :] = o
            l_cm[c, _rows(n, bb), :] = jnp.broadcast_to(lse, (bb, LANES))
        return emit

    def to_sequence(r, cm, sq):
        for c in range(r):
            sq[pl.ds(c, seq // r, stride=r), :] = cm[c]

    pattern(2, 16, q16_ref, k16_ref, v16_ref, class_major(c16o_ref, c16l_ref))
    to_sequence(16, c16o_ref, s16o_ref)
    to_sequence(16, c16l_ref, s16l_ref)
    pattern(1, 4, q4_ref, k4_ref, v4_ref, class_major(c4o_ref, c4l_ref))
    to_sequence(4, c4o_ref, s4o_ref)
    to_sequence(4, c4l_ref, s4l_ref)

    def merge(c, n, o1, l1):
        rows = _rows(n, bb)
        l4, l16 = s4l_ref[rows, :], s16l_ref[rows, :]
        top = jnp.maximum(jnp.maximum(l1, l4), l16)
        e1, e4, e16 = jnp.exp2(l1 - top), jnp.exp2(l4 - top), jnp.exp2(l16 - top)
        mix = e1 * o1 + e4 * s4o_ref[rows, :] + e16 * s16o_ref[rows, :]
        o_ref[0, rows, :] = (mix / (e1 + e4 + e16)).astype(BF16)

    pattern(0, 1, q1_ref, k1_ref, v1_ref, merge)


def _dilated_attention(d1, d4, d16, tabs):
    B, S = d1.shape[0], d1.shape[3]
    assert [r for _, r in DILATIONS] == [1, 4, 16] and all(w // r == BAND_BLOCK for w, r in DILATIONS)
    assert (S // BAND_BLOCK) % DIL_UNROLL == 0
    head = lambda r, part: pl.BlockSpec((1, 1, r, S // r, LANES), lambda b, h: (b, part * H_C + h, 0, 0, 0))
    class_major = lambda r: pltpu.VMEM((r, S // r, LANES), F32)
    return pl.pallas_call(
        functools.partial(_dilated_kernel, seq=S),
        grid=(B, H_C),
        in_specs=[head(r, part) for r in (1, 4, 16) for part in range(3)]
        + [pl.BlockSpec((1, len(DILATIONS), 2 * BAND_BLOCK), lambda b, h: (h, 0, 0))],
        out_specs=pl.BlockSpec((1, S, LANES), lambda b, h: (b, 0, h)),
        out_shape=jax.ShapeDtypeStruct((B, S, H_C * HEAD_DIM), BF16),
        scratch_shapes=[class_major(4), class_major(4), class_major(16), class_major(16)]
        + [pltpu.VMEM((S, LANES), F32)] * 4,
        compiler_params=_params(("arbitrary", "arbitrary")),
        name="dilated_attention",
    )(d1, d1, d1, d4, d4, d4, d16, d16, d16, tabs)


def _merge_kernel(oa_ref, ob_ref, oc_ref, g0_ref, g1_ref, g2_ref, wbr_ref, wo_ref, x_ref, out_ref):
    ea, eb = H_A * HEAD_DIM, (H_A + H_B) * HEAD_DIM
    merged = (g0_ref[0].astype(F32) * _dot(oa_ref[0], wbr_ref[:ea, :])
              + g1_ref[0].astype(F32) * _dot(ob_ref[0], wbr_ref[ea:eb, :])
              + g2_ref[0].astype(F32) * _dot(oc_ref[0], wbr_ref[eb:, :]))
    out_ref[0] = x_ref[0] + _dot(merged.astype(BF16), wo_ref[...])


def _merge_out(oa, ob, oc, gates, w_branch, wo, x, *, tm):
    B, S, D = x.shape
    rows = lambda w: pl.BlockSpec((1, tm, w), lambda b, s: (b, s, 0))
    gate = lambda g: pl.BlockSpec((1, tm, D), lambda b, s: (b, s, g))
    resident = lambda w: pl.BlockSpec(w.shape, lambda b, s: (0, 0), pipeline_mode=pl.Buffered(1))
    return pl.pallas_call(
        _merge_kernel,
        grid=(B, S // tm),
        in_specs=[rows(oa.shape[-1]), rows(ob.shape[-1]), rows(oc.shape[-1]), gate(0), gate(1), gate(2),
                  resident(w_branch), resident(wo), rows(D)],
        out_specs=rows(D),
        out_shape=jax.ShapeDtypeStruct((B, S, D), F32),
        compiler_params=_params(("arbitrary", "arbitrary")),
        name="merge_out",
    )(oa, ob, oc, gates, gates, gates, w_branch, wo, x)


FFN_SUB = 256


def _ffn_kernel(x_ref, g_ref, wg_ref, wu_ref, wd_ref, o_ref, h_ref):
    @pl.when(pl.program_id(2) == 0)
    def _():
        x = x_ref[0]
        h_ref[...] = (_rms_rows(x) * g_ref[...]).astype(BF16)
        o_ref[0] = x

    for mi in range(h_ref.shape[0] // FFN_SUB):
        rows = slice(mi * FFN_SUB, (mi + 1) * FFN_SUB)
        h = h_ref[rows, :]
        gate = _dot(h, wg_ref[...])
        up = _dot(h, wu_ref[...])
        act = (gate * jax.nn.sigmoid(gate) * up).astype(BF16)
        o_ref[0, rows, :] += _dot(act, wd_ref[...])


def _ffn(x, g, w_gu, w_down, *, tm, tf):
    B, S, D = x.shape
    nf = D_FF // tf
    return pl.pallas_call(
        _ffn_kernel,
        grid=(B, S // tm, nf),
        in_specs=[
            pl.BlockSpec((1, tm, D), lambda b, s, f: (b, s, 0)),
            pl.BlockSpec((1, D), lambda b, s, f: (0, 0)),
            pl.BlockSpec((D, tf), lambda b, s, f: (0, f)),
            pl.BlockSpec((D, tf), lambda b, s, f: (0, nf + f)),
            pl.BlockSpec((tf, D), lambda b, s, f: (f, 0)),
        ],
        out_specs=pl.BlockSpec((1, tm, D), lambda b, s, f: (b, s, 0)),
        out_shape=jax.ShapeDtypeStruct((B, S, D), F32),
        scratch_shapes=[pltpu.VMEM((tm, D), BF16)],
        compiler_params=_params(("arbitrary", "arbitrary", "arbitrary")),
        name="ffn",
    )(x, g, w_gu, w_gu, w_down)


def _t5_bucket(dist):
    max_exact = N_BUCKETS // 2
    d = jnp.maximum(dist, 0)
    df = jnp.maximum(d, 1).astype(F32)
    large = max_exact + (jnp.log(df / max_exact) / math.log(MAX_DISTANCE / max_exact)
                         * (N_BUCKETS - max_exact)).astype(jnp.int32)
    large = jnp.minimum(large, N_BUCKETS - 1)
    return jnp.where(d < max_exact, d, large)


def _bias_tables(rel_bias, S):
    by_dist = rel_bias[_t5_bucket(jnp.arange(S))].T.astype(F32) * LOG2E
    nblk, blk, bb = S // MOBA_BLOCK, MOBA_BLOCK, BAND_BLOCK
    rev = jnp.pad(by_dist[:H_B], ((0, 0), (blk, 0)))[:, ::-1]
    tab_b = jnp.stack([rev[:, S - blk * (d + 1): S - blk * (d - 1)] for d in range(nblk)], axis=1)
    tabs_c = []
    for _, r in DILATIONS:
        near = by_dist[H_B:, ::r][:, :bb + 1]
        tabs_c.append(jnp.pad(near[:, ::-1], ((0, 0), (0, bb - 1))))
    return tab_b, jnp.stack(tabs_c, axis=1)


def _proj_weights(w_in, q_gain, k_gain):
    scale = HEAD_DIM ** -0.5
    ones = lambda n: jnp.ones((n * HEAD_DIM,), F32)
    hb = H_A + H_B

    def cols(part, h0, h1):
        return w_in[:, part * MIX_WIDTH + h0 * HEAD_DIM: part * MIX_WIDTH + h1 * HEAD_DIM]

    qg, kg = q_gain.astype(F32) * (scale * LOG2E), k_gain.astype(F32)
    w = jnp.concatenate(
        [cols(p, 0, H_A) for p in range(3)] + [cols(p, H_A, hb) for p in range(3)]
        + [cols(p, hb, N_HEADS) for p in range(3)] + [w_in[:, 3 * MIX_WIDTH:]], axis=1).astype(BF16)
    colgain = jnp.concatenate([
        ones(H_A) * scale, ones(H_A), ones(H_A),
        qg[:H_B].reshape(-1), kg[:H_B].reshape(-1), ones(H_B),
        qg[H_B:].reshape(-1), kg[H_B:].reshape(-1), ones(H_C)])[None, :]
    return w, colgain


def _layer(x, g_mix, w_in, q_gain, k_gain, w_branch, w_out, g_ffn, w_gu, w_down, tab_b, tabs_c):
    w, colgain = _proj_weights(w_in, q_gain, k_gain)
    qkv, d1, d4, d16, gates = _proj(x, g_mix[None, :], w, colgain, tm=1024)

    oa = _sb_attention(qkv, blk=256)
    ob = _moba_attention(qkv, tab_b)
    oc = _dilated_attention(d1, d4, d16, tabs_c)

    x = _merge_out(oa, ob, oc, gates, w_branch.astype(BF16), w_out.astype(BF16), x, tm=256)
    return _ffn(x, g_ffn[None, :], w_gu.astype(BF16), w_down.astype(BF16), tm=1024, tf=512)


def kernel(x, g_mix, w_in, q_gain, k_gain, w_branch, w_out, g_ffn, w_gu, w_down, rel_bias):
    depth = g_mix.shape[0]
    tab_b, tabs_c = _bias_tables(rel_bias, x.shape[1])
    for l in range(depth):
        x = _layer(x, g_mix[l], w_in[l], q_gain[l], k_gain[l], w_branch[l], w_out[l],
                   g_ffn[l], w_gu[l], w_down[l], tab_b, tabs_c)
    return x
```

```python
import functools
import math

import jax
import jax.numpy as jnp
from jax import lax
from jax.experimental import pallas as pl
from jax.experimental.pallas import tpu as pltpu

D_MODEL = 2048
HEAD_DIM = 128
H_A = 4
H_B = 6
H_C = 6
N_HEADS = H_A + H_B + H_C
MIX_WIDTH = N_HEADS * HEAD_DIM
N_BRANCH = 3
D_FF = 5632
MOBA_BLOCK = 256
MOBA_TOPK = 3
DILATIONS = ((128, 1), (512, 4), (2048, 16))
BAND_BLOCK = 128
N_BUCKETS = 32
MAX_DISTANCE = 2048
RMS_EPS = 1e-6
NEG_INF = -1e30
LOG2E = math.log2(math.e)

LANES = 128
VMEM_LIMIT = 56 * 1024 * 1024

SB_SKIP_SUM = 110.0

F32 = jnp.float32
BF16 = jnp.bfloat16

SLOTS_MAIN = 3 * (H_A + H_B)
SLOTS_DIL = 3 * H_C
SB_Q, SB_K, SB_V = 0, H_A, 2 * H_A
MB_Q, MB_K, MB_V = 3 * H_A, 3 * H_A + H_B, 3 * H_A + 2 * H_B


def _params(sem):
    return pltpu.CompilerParams(dimension_semantics=sem, vmem_limit_bytes=VMEM_LIMIT)


def _dot(a, b):
    return jnp.dot(a, b, preferred_element_type=F32)


def _dot_nt(a, b):
    return lax.dot_general(a, b, (((1,), (1,)), ((), ())), preferred_element_type=F32)


def _rows(n, size, length=None):
    start = n * size if isinstance(n, int) else pl.multiple_of(n * size, size)
    return pl.ds(start, size if length is None else length)


def _rms_rows(x):
    return x * lax.rsqrt(jnp.mean(x * x, axis=-1, keepdims=True) + RMS_EPS)


PROJ_HEADS = 6
PROJ_TN = PROJ_HEADS * LANES
J_MAIN = SLOTS_MAIN // PROJ_HEADS
J_DIL = SLOTS_DIL // PROJ_HEADS
J_GATE = N_BRANCH * D_MODEL // PROJ_TN
NORMED_TILES = (2, 3, J_MAIN, J_MAIN + 1)
PROJ_SUB = 256


def _proj_kernel(x_ref, g_ref, w_ref, cg_ref, main_ref, d1_ref, d4_ref, d16_ref, gates_ref, h_ref, y_ref, y4_ref):
    j = pl.program_id(2)
    tm = h_ref.shape[0]
    sub = PROJ_SUB

    @pl.when(j == 0)
    def _():
        h_ref[...] = (_rms_rows(x_ref[0]) * g_ref[...]).astype(BF16)

    is_gate = j >= J_MAIN + J_DIL
    is_dil = (j >= J_MAIN) & jnp.logical_not(is_gate)
    is_norm = functools.reduce(jnp.logical_or, [j == t for t in NORMED_TILES])

    def sweep(epilogue):
        for mi in range(tm // sub):
            rows = slice(mi * sub, (mi + 1) * sub)
            epilogue(mi, rows, _dot(h_ref[rows, :], w_ref[...]))

    def heads(acc, normed):
        for hh in range(PROJ_HEADS):
            sl = slice(hh * LANES, (hh + 1) * LANES)
            yield hh, (_rms_rows(acc[:, sl]) if normed else acc[:, sl]) * cg_ref[:, sl]

    @pl.when(is_gate)
    def _():
        def epilogue(mi, rows, acc):
            gates_ref[0, rows, :] = jax.nn.sigmoid(acc).astype(BF16)
        sweep(epilogue)

    for normed in (False, True):
        @pl.when((j < J_MAIN) & (is_norm == normed))
        def _():
            def epilogue(mi, rows, acc):
                for hh, y in heads(acc, normed):
                    main_ref[0, hh, rows, :] = y.astype(BF16)
            sweep(epilogue)

        @pl.when(is_dil & (is_norm == normed))
        def _():
            def epilogue(mi, rows, acc):
                for hh, y in heads(acc, normed):
                    d1_ref[0, hh, 0, rows, :] = y.astype(BF16)
                    y_ref[mi % 2, hh] = y
                    for c in range(4):
                        y4 = y_ref[mi % 2, hh, pl.ds(c, sub // 4, stride=4), :]
                        d4_ref[0, hh, c, mi * (sub // 4):(mi + 1) * (sub // 4), :] = y4.astype(BF16)
                        y4_ref[mi % 2, hh, c] = y4
                        for k in range(4):
                            d16_ref[0, hh, c + 4 * k, mi * (sub // 16):(mi + 1) * (sub // 16), :] = (
                                y4_ref[mi % 2, hh, c, pl.ds(k, sub // 16, stride=4), :].astype(BF16))
            sweep(epilogue)


def _proj(x, g, w, colgain, *, tm):
    B, S, D = x.shape
    tn, hp = PROJ_TN, PROJ_HEADS
    jd, jg = J_MAIN, J_MAIN + J_DIL
    dil = lambda r: pl.BlockSpec((1, hp, r, tm // r, LANES),
                                 lambda b, s, j: (b, jnp.clip(j - jd, 0, J_DIL - 1), 0, s, 0))
    dil_shape = lambda r: jax.ShapeDtypeStruct((B, SLOTS_DIL, r, S // r, LANES), BF16)
    return pl.pallas_call(
        _proj_kernel,
        grid=(B, S // tm, J_MAIN + J_DIL + J_GATE),
        in_specs=[
            pl.BlockSpec((1, tm, D), lambda b, s, j: (b, s, 0)),
            pl.BlockSpec((1, D), lambda b, s, j: (0, 0)),
            pl.BlockSpec((D, tn), lambda b, s, j: (0, j)),
            pl.BlockSpec((1, tn), lambda b, s, j: (0, jnp.minimum(j, jg - 1))),
        ],
        out_specs=[
            pl.BlockSpec((1, hp, tm, LANES), lambda b, s, j: (b, jnp.minimum(j, jd - 1), s, 0)),
            dil(1), dil(4), dil(16),
            pl.BlockSpec((1, tm, tn), lambda b, s, j: (b, s, jnp.maximum(j - jg, 0))),
        ],
        out_shape=[
            jax.ShapeDtypeStruct((B, SLOTS_MAIN, S, LANES), BF16),
            dil_shape(1), dil_shape(4), dil_shape(16),
            jax.ShapeDtypeStruct((B, S, N_BRANCH * D_MODEL), BF16),
        ],
        scratch_shapes=[pltpu.VMEM((tm, D), BF16), pltpu.VMEM((2, hp, PROJ_SUB, LANES), F32),
                        pltpu.VMEM((2, hp, 4, PROJ_SUB // 4, LANES), F32)],
        compiler_params=_params(("arbitrary", "arbitrary", "arbitrary")),
        name="proj",
    )(x, g, w, colgain)


def _softplus2(z):
    return jnp.maximum(z, 0.0) + jnp.log2(1.0 + jnp.exp2(-jnp.abs(z)))


def _sb_kernel(q_ref, k_ref, v_ref, u_ref, o_ref, *, blk):
    i = pl.program_id(1)
    u = u_ref[...]
    row = lax.broadcasted_iota(jnp.int32, (blk, blk), 0)
    col = lax.broadcasted_iota(jnp.int32, (blk, blk), 1)
    past = col < row

    def blocks(j, state, diag):
        keys = pl.ds(pl.multiple_of(j * blk, blk), blk)
        heads = range(H_A)
        z = [_dot_nt(q_ref[0, h], k_ref[0, h, keys, :]) for h in heads]
        sp = [_softplus2(z[h]) for h in heads]
        spm = [jnp.where(past, sp[h], 0.0) for h in heads] if diag else sp
        hi = [spm[h].astype(BF16) for h in heads]
        lo = [(spm[h] - hi[h].astype(F32)).astype(BF16) for h in heads]
        after = [_dot(hi[h], u) + _dot(lo[h], u) for h in heads]
        w = [jnp.exp2(z[h] - sp[h] - after[h] - state[h][0]) for h in heads]
        if diag:
            w = [jnp.where(past, w[h], 0.0) for h in heads]
        acc = [state[h][1] + _dot(w[h].astype(BF16), v_ref[0, h, keys, :]) for h in heads]
        c = [state[h][0] + after[h][:, :1] + spm[h][:, :1] for h in heads]
        return tuple(zip(c, acc))

    def smallest(state):
        return functools.reduce(jnp.minimum, [jnp.min(c) for c, _ in state])

    zeros = (jnp.zeros((blk, 1), F32), jnp.zeros((blk, HEAD_DIM), F32))
    state = blocks(i, (zeros,) * H_A, True)

    def cond(carry):
        j, c_min, _ = carry
        return (j >= 0) & (c_min < SB_SKIP_SUM * LOG2E)

    def body(carry):
        j, _, state = carry
        state = blocks(j, state, False)
        return j - 1, smallest(state), state

    _, _, state = lax.while_loop(cond, body, (i - 1, smallest(state), state))
    for h, (_, acc) in enumerate(state):
        o_ref[0, :, h * LANES:(h + 1) * LANES] = acc.astype(BF16)


def _sb_attention(qkv, *, blk):
    B, _, S, _ = qkv.shape
    tri = (jnp.arange(blk)[:, None] > jnp.arange(blk)[None, :]).astype(BF16)
    return pl.pallas_call(
        functools.partial(_sb_kernel, blk=blk),
        grid=(B, S // blk),
        in_specs=[
            pl.BlockSpec((1, H_A, blk, LANES), lambda b, i: (b, SB_Q // H_A, i, 0)),
            pl.BlockSpec((1, H_A, S, LANES), lambda b, i: (b, SB_K // H_A, 0, 0)),
            pl.BlockSpec((1, H_A, S, LANES), lambda b, i: (b, SB_V // H_A, 0, 0)),
            pl.BlockSpec((blk, blk), lambda b, i: (0, 0)),
        ],
        out_specs=pl.BlockSpec((1, blk, H_A * LANES), lambda b, i: (b, i, 0)),
        out_shape=jax.ShapeDtypeStruct((B, S, H_A * HEAD_DIM), BF16),
        compiler_params=_params(("arbitrary", "arbitrary")),
        name="sb_attention",
    )(qkv, qkv, qkv, tri)


def _toeplitz(row_vals, n_rows, shift):
    full = jnp.broadcast_to(row_vals, (n_rows, row_vals.shape[1]))
    return pltpu.roll(full, shift, 1, stride=1, stride_axis=0)


MOBA_STRIP = 32
GATE_UNROLL = 4


def _moba_kernel(q_ref, k_ref, v_ref, tab_ref, o_ref, tb_ref, ob_ref, ka_ref, va_ref, qa_ref, m_ref, accl_ref, sc_ref,
                 p_ref, al_ref, *, nblk, n_sel):
    blk = MOBA_BLOCK
    wide = 2 * blk
    npair = nblk // 2

    row = lax.broadcasted_iota(jnp.int32, (blk, blk), 0)
    col = lax.broadcasted_iota(jnp.int32, (blk, blk), 1)

    @pl.when(pl.program_id(1) == 0)
    def _():
        for d in range(nblk):
            tb_ref[d] = _toeplitz(tab_ref[0, d:d + 1, :], blk, blk + 1)[:, :blk]
        diag = jnp.where(col <= row, tb_ref[0], NEG_INF)
        ob_ref[:blk, :blk] = diag
        ob_ref[:blk, blk:] = jnp.zeros((blk, blk), F32)
        ob_ref[blk:, :blk] = tb_ref[1]
        ob_ref[blk:, blk:] = diag

    lane = lax.broadcasted_iota(jnp.int32, (blk, LANES), 1)
    va_ref[:, LANES:] = jnp.ones((va_ref.shape[0], LANES), BF16)
    means = []
    for n in range(nblk):
        rows = slice(n * blk, (n + 1) * blk)
        kb = k_ref[0, 0, rows, :]
        means.append(jnp.mean(kb.astype(F32), axis=0, keepdims=True))
        ka_ref[rows, :LANES] = kb
        ka_ref[rows, LANES:] = jnp.where(lane == n, 1.0, 0.0).astype(BF16)
        va_ref[rows, :LANES] = v_ref[0, 0, rows, :]
    km = jnp.concatenate(means, axis=0).astype(BF16)

    blk_id = lax.broadcasted_iota(jnp.int32, (nblk, blk), 0)
    eye = jnp.where(row == col, 1.0, 0.0).astype(BF16)

    def gating(g, _):
        ids = [g * GATE_UNROLL + k for k in range(GATE_UNROLL)]
        windows = [pl.ds(pl.multiple_of(g * (GATE_UNROLL * blk), GATE_UNROLL * blk) + k * blk, blk)
                   for k in range(GATE_UNROLL)]
        q = [q_ref[0, 0, rows, :] for rows in windows]
        fully_past = [blk_id < i for i in ids]
        gate = [jnp.where(fp, _dot_nt(km, x), NEG_INF) for fp, x in zip(fully_past, q)]
        sel = [jnp.zeros((nblk, blk), F32) for _ in ids]
        for _ in range(n_sel):
            top = [jnp.max(x, axis=0, keepdims=True) for x in gate]
            first = [jnp.min(jnp.where(x == t, blk_id, nblk), axis=0, keepdims=True)
                     for x, t in zip(gate, top)]
            pick = [blk_id == f for f in first]
            sel = [jnp.where(pk, 1.0, x) for pk, x in zip(pick, sel)]
            gate = [jnp.where(pk, -jnp.inf, x) for pk, x in zip(pick, gate)]
        pad = jnp.zeros((LANES - nblk, blk), BF16)
        pen_t = [jnp.concatenate([jnp.where((blk_id == i) | (fp & (x > 0.0)), 0.0, NEG_INF).astype(BF16), pad], axis=0)
                 for i, fp, x in zip(ids, fully_past, sel)]
        pen = [_dot_nt(eye, x).astype(BF16) for x in pen_t]
        for rows, x, y in zip(windows, q, pen):
            qa_ref[rows, :LANES] = x
            qa_ref[rows, LANES:] = y
        return 0

    lax.fori_loop(0, nblk // GATE_UNROLL, gating, 0)


    def aligned(x, m):
        return x if isinstance(x, int) else pl.multiple_of(x, m)

    def pair_bias(d):
        return jnp.concatenate([jnp.concatenate([tb_ref[d], tb_ref[d - 1]], axis=1),
                                jnp.concatenate([tb_ref[d + 1], tb_ref[d]], axis=1)], axis=0)

    def soften(t, j, slot):
        s_base = aligned(j * wide, wide)
        q_base = s_base if t is None else aligned((t + 1 + j) * wide, wide)
        for st in range(wide // MOBA_STRIP):
            off = st * MOBA_STRIP
            s = sc_ref[pl.ds(s_base + off, MOBA_STRIP), :]
            state = pl.ds(q_base + off, MOBA_STRIP)
            m = jnp.broadcast_to(jnp.max(s, axis=-1, keepdims=True), (MOBA_STRIP, LANES))
            if t is not None:
                m_old = m_ref[state, :]
                m = jnp.maximum(m_old, m)
                al_ref[slot, off:off + MOBA_STRIP, :] = jnp.exp2(m_old - m)
            p_ref[slot, off:off + MOBA_STRIP, :] = jnp.exp2(s - jnp.concatenate([m] * (wide // LANES), axis=1)).astype(BF16)
            m_ref[state, :] = m

    def accumulate(t, j, slot):
        if t is None:
            rows = _rows(j, wide)
            accl_ref[rows, :] = _dot(p_ref[slot], va_ref[rows, :])
        else:
            rows = _rows(t + 1 + j, wide)
            al = al_ref[slot]
            accl_ref[rows, :] = (jnp.concatenate([al, al], axis=1) * accl_ref[rows, :]
                                 + _dot(p_ref[slot], va_ref[t * wide:(t + 1) * wide, :]))

    def phase(t, n):
        soften(t, 0, 0)

        def body(j, _):
            accumulate(t, j - 1, (j - 1) % 2)
            soften(t, j, j % 2)
            return 0

        lax.fori_loop(1, n, body, 0)
        accumulate(t, n - 1, (n - 1) % 2)

    for u in range(npair):
        rows = slice(u * wide, (u + 1) * wide)
        sc_ref[rows, :] = _dot_nt(qa_ref[rows, :], ka_ref[rows, :]) + ob_ref[...]
    phase(None, npair)
    for t in range(npair - 1):
        n = npair - 1 - t
        for j in range(n):
            rows = slice((t + 1 + j) * wide, (t + 2 + j) * wide)
            sc_ref[j * wide:(j + 1) * wide, :] = (_dot_nt(qa_ref[rows, :], ka_ref[t * wide:(t + 1) * wide, :])
                                                  + pair_bias(2 + 2 * j))
        phase(t, n)


    def finish(i, _):
        rows = _rows(i, blk)
        o_ref[0, rows, :] = (accl_ref[rows, :LANES] / accl_ref[rows, LANES:]).astype(BF16)
        return 0

    lax.fori_loop(0, nblk, finish, 0)


def _moba_attention(qkv, tab):
    B, _, S, _ = qkv.shape
    blk = MOBA_BLOCK
    nblk = S // blk
    assert nblk % 2 == 0 and nblk % GATE_UNROLL == 0 and nblk <= LANES
    n_sel = min(MOBA_TOPK, nblk - 1)
    head = lambda slot: pl.BlockSpec((1, 1, S, LANES), lambda h, b: (b, slot + h, 0, 0))
    return pl.pallas_call(
        functools.partial(_moba_kernel, nblk=nblk, n_sel=n_sel),
        grid=(H_B, B),
        in_specs=[head(MB_Q), head(MB_K), head(MB_V), pl.BlockSpec((1, nblk, 2 * blk), lambda h, b: (h, 0, 0))],
        out_specs=pl.BlockSpec((1, S, LANES), lambda h, b: (b, 0, h)),
        out_shape=jax.ShapeDtypeStruct((B, S, H_B * HEAD_DIM), BF16),
        scratch_shapes=[pltpu.VMEM((nblk, blk, blk), F32), pltpu.VMEM((2 * blk, 2 * blk), F32),
                        pltpu.VMEM((S, 2 * LANES), BF16), pltpu.VMEM((S, 2 * LANES), BF16),
                        pltpu.VMEM((S, 2 * LANES), BF16), pltpu.VMEM((S, LANES), F32),
                        pltpu.VMEM((S, 2 * LANES), F32), pltpu.VMEM((S, 2 * blk), F32),
                        pltpu.VMEM((2, 2 * blk, 2 * blk), BF16), pltpu.VMEM((2, 2 * blk, LANES), F32)],
        compiler_params=_params(("arbitrary", "arbitrary")),
        name="moba_attention",
    )(qkv, qkv, qkv, tab)


DIL_UNROLL = 8


def _dilated_kernel(q1_ref, k1_ref, v1_ref, q4_ref, k4_ref, v4_ref, q16_ref, k16_ref, v16_ref, tab_ref, o_ref,
                    c4o_ref, c4l_ref, c16o_ref, c16l_ref, s4o_ref, s4l_ref, s16o_ref, s16l_ref, *, seq):
    bb = BAND_BLOCK
    qi = lax.broadcasted_iota(jnp.int32, (bb, 2 * bb), 0)
    kj = lax.broadcasted_iota(jnp.int32, (bb, 2 * bb), 1)
    visible = (kj >= qi) & (kj <= qi + bb)

    def pattern(g, r, q_ref, k_ref, v_ref, emit):
        bias = jnp.where(visible, _toeplitz(tab_ref[0, g:g + 1, :], bb, 0), NEG_INF)
        bias_first = jnp.concatenate([bias[:, bb:], jnp.full((bb, bb), NEG_INF, F32)], axis=1)
        nb = seq // r // bb

        def body(t, _):
            blocks = [((t * DIL_UNROLL + k) // nb, (t * DIL_UNROLL + k) % nb) for k in range(DIL_UNROLL)]
            keys = [_rows(jnp.maximum(n - 1, 0), bb, 2 * bb) for _, n in blocks]
            s = [_dot_nt(q_ref[0, 0, c, _rows(n, bb), :], k_ref[0, 0, c, kw, :]) + jnp.where(n == 0, bias_first, bias)
                 for (c, n), kw in zip(blocks, keys)]
            m = [jnp.max(x, axis=-1, keepdims=True) for x in s]
            p = [jnp.exp2(x - mx) for x, mx in zip(s, m)]
            den = [jnp.sum(x, axis=-1, keepdims=True) for x in p]
            o = [_dot(x.astype(BF16), v_ref[0, 0, c, kw, :]) for x, (c, _), kw in zip(p, blocks, keys)]
            for (c, n), ox, dx, mx in zip(blocks, o, den, m):
                emit(c, n, ox / dx, mx + jnp.log2(dx))
            return 0

        lax.fori_loop(0, r * nb // DIL_UNROLL, body, 0)

    def class_major(o_cm, l_cm):
        def emit(c, n, o, lse):
            o_cm[c, _rows(n, bb), :] = o
            l_cm[c, _rows(n, bb), :] = jnp.broadcast_to(lse, (bb, LANES))
        return emit

    def to_sequence(cm4, sq):
        for c in range(4):
            sq[pl.ds(c, seq // 4, stride=4), :] = cm4[c]

    def to_classes_of_4(cm16, cm4):
        for c in range(4):
            for k in range(4):
                cm4[c, pl.ds(k, seq // 16, stride=4), :] = cm16[c + 4 * k]

    pattern(2, 16, q16_ref, k16_ref, v16_ref, class_major(c16o_ref, c16l_ref))
    for cm16, sq in ((c16o_ref, s16o_ref), (c16l_ref, s16l_ref)):
        to_classes_of_4(cm16, c4o_ref)
        to_sequence(c4o_ref, sq)
    pattern(1, 4, q4_ref, k4_ref, v4_ref, class_major(c4o_ref, c4l_ref))
    to_sequence(c4o_ref, s4o_ref)
    to_sequence(c4l_ref, s4l_ref)

    def merge(c, n, o1, l1):
        rows = _rows(n, bb)
        l4, l16 = s4l_ref[rows, :], s16l_ref[rows, :]
        top = jnp.maximum(jnp.maximum(l1, l4), l16)
        e1, e4, e16 = jnp.exp2(l1 - top), jnp.exp2(l4 - top), jnp.exp2(l16 - top)
        mix = e1 * o1 + e4 * s4o_ref[rows, :] + e16 * s16o_ref[rows, :]
        o_ref[0, rows, :] = (mix / (e1 + e4 + e16)).astype(BF16)

    pattern(0, 1, q1_ref, k1_ref, v1_ref, merge)


def _dilated_attention(d1, d4, d16, tabs):
    B, S = d1.shape[0], d1.shape[3]
    assert [r for _, r in DILATIONS] == [1, 4, 16] and all(w // r == BAND_BLOCK for w, r in DILATIONS)
    assert (S // BAND_BLOCK) % DIL_UNROLL == 0
    head = lambda r, part: pl.BlockSpec((1, 1, r, S // r, LANES), lambda b, h: (b, part * H_C + h, 0, 0, 0))
    class_major = lambda r: pltpu.VMEM((r, S // r, LANES), F32)
    return pl.pallas_call(
        functools.partial(_dilated_kernel, seq=S),
        grid=(B, H_C),
        in_specs=[head(r, part) for r in (1, 4, 16) for part in range(3)]
        + [pl.BlockSpec((1, len(DILATIONS), 2 * BAND_BLOCK), lambda b, h: (h, 0, 0))],
        out_specs=pl.BlockSpec((1, S, LANES), lambda b, h: (b, 0, h)),
        out_shape=jax.ShapeDtypeStruct((B, S, H_C * HEAD_DIM), BF16),
        scratch_shapes=[class_major(4), class_major(4), class_major(16), class_major(16)]
        + [pltpu.VMEM((S, LANES), F32)] * 4,
        compiler_params=_params(("arbitrary", "arbitrary")),
        name="dilated_attention",
    )(d1, d1, d1, d4, d4, d4, d16, d16, d16, tabs)


def _merge_kernel(oa_ref, ob_ref, oc_ref, g0_ref, g1_ref, g2_ref, wbr_ref, wo_ref, x_ref, out_ref):
    ea, eb = H_A * HEAD_DIM, (H_A + H_B) * HEAD_DIM
    merged = (g0_ref[0].astype(F32) * _dot(oa_ref[0], wbr_ref[:ea, :])
              + g1_ref[0].astype(F32) * _dot(ob_ref[0], wbr_ref[ea:eb, :])
              + g2_ref[0].astype(F32) * _dot(oc_ref[0], wbr_ref[eb:, :]))
    out_ref[0] = x_ref[0] + _dot(merged.astype(BF16), wo_ref[...])


def _merge_out(oa, ob, oc, gates, w_branch, wo, x, *, tm):
    B, S, D = x.shape
    rows = lambda w: pl.BlockSpec((1, tm, w), lambda b, s: (b, s, 0))
    gate = lambda g: pl.BlockSpec((1, tm, D), lambda b, s: (b, s, g))
    resident = lambda w: pl.BlockSpec(w.shape, lambda b, s: (0, 0), pipeline_mode=pl.Buffered(1))
    return pl.pallas_call(
        _merge_kernel,
        grid=(B, S // tm),
        in_specs=[rows(oa.shape[-1]), rows(ob.shape[-1]), rows(oc.shape[-1]), gate(0), gate(1), gate(2),
                  resident(w_branch), resident(wo), rows(D)],
        out_specs=rows(D),
        out_shape=jax.ShapeDtypeStruct((B, S, D), F32),
        compiler_params=_params(("arbitrary", "arbitrary")),
        name="merge_out",
    )(oa, ob, oc, gates, gates, gates, w_branch, wo, x)


FFN_SUB = 256


def _ffn_kernel(x_ref, g_ref, wg_ref, wu_ref, wd_ref, o_ref, h_ref):
    @pl.when(pl.program_id(2) == 0)
    def _():
        x = x_ref[0]
        h_ref[...] = (_rms_rows(x) * g_ref[...]).astype(BF16)
        o_ref[0] = x

    for mi in range(h_ref.shape[0] // FFN_SUB):
        rows = slice(mi * FFN_SUB, (mi + 1) * FFN_SUB)
        h = h_ref[rows, :]
        gate = _dot(h, wg_ref[...])
        up = _dot(h, wu_ref[...])
        act = (gate * jax.nn.sigmoid(gate) * up).astype(BF16)
        o_ref[0, rows, :] += _dot(act, wd_ref[...])


def _ffn(x, g, w_gu, w_down, *, tm, tf):
    B, S, D = x.shape
    nf = D_FF // tf
    return pl.pallas_call(
        _ffn_kernel,
        grid=(B, S // tm, nf),
        in_specs=[
            pl.BlockSpec((1, tm, D), lambda b, s, f: (b, s, 0)),
            pl.BlockSpec((1, D), lambda b, s, f: (0, 0)),
            pl.BlockSpec((D, tf), lambda b, s, f: (0, f)),
            pl.BlockSpec((D, tf), lambda b, s, f: (0, nf + f)),
            pl.BlockSpec((tf, D), lambda b, s, f: (f, 0)),
        ],
        out_specs=pl.BlockSpec((1, tm, D), lambda b, s, f: (b, s, 0)),
        out_shape=jax.ShapeDtypeStruct((B, S, D), F32),
        scratch_shapes=[pltpu.VMEM((tm, D), BF16)],
        compiler_params=_params(("arbitrary", "arbitrary", "arbitrary")),
        name="ffn",
    )(x, g, w_gu, w_gu, w_down)


def _t5_bucket(dist):
    max_exact = N_BUCKETS // 2
    d = jnp.maximum(dist, 0)
    df = jnp.maximum(d, 1).astype(F32)
    large = max_exact + (jnp.log(df / max_exact) / math.log(MAX_DISTANCE / max_exact)
                         * (N_BUCKETS - max_exact)).astype(jnp.int32)
    large = jnp.minimum(large, N_BUCKETS - 1)
    return jnp.where(d < max_exact, d, large)


def _bias_tables(rel_bias, S):
    by_dist = rel_bias[_t5_bucket(jnp.arange(S))].T.astype(F32) * LOG2E
    nblk, blk, bb = S // MOBA_BLOCK, MOBA_BLOCK, BAND_BLOCK
    rev = jnp.pad(by_dist[:H_B], ((0, 0), (blk, 0)))[:, ::-1]
    tab_b = jnp.stack([rev[:, S - blk * (d + 1): S - blk * (d - 1)] for d in range(nblk)], axis=1)
    tabs_c = []
    for _, r in DILATIONS:
        near = by_dist[H_B:, ::r][:, :bb + 1]
        tabs_c.append(jnp.pad(near[:, ::-1], ((0, 0), (0, bb - 1))))
    return tab_b, jnp.stack(tabs_c, axis=1)


def _proj_weights(w_in, q_gain, k_gain):
    scale = HEAD_DIM ** -0.5
    ones = lambda n: jnp.ones((n * HEAD_DIM,), F32)
    hb = H_A + H_B

    def cols(part, h0, h1):
        return w_in[:, part * MIX_WIDTH + h0 * HEAD_DIM: part * MIX_WIDTH + h1 * HEAD_DIM]

    qg, kg = q_gain.astype(F32) * (scale * LOG2E), k_gain.astype(F32)
    w = jnp.concatenate(
        [cols(p, 0, H_A) for p in range(3)] + [cols(p, H_A, hb) for p in range(3)]
        + [cols(p, hb, N_HEADS) for p in range(3)] + [w_in[:, 3 * MIX_WIDTH:]], axis=1).astype(BF16)
    colgain = jnp.concatenate([
        ones(H_A) * (scale * LOG2E), ones(H_A), ones(H_A),
        qg[:H_B].reshape(-1), kg[:H_B].reshape(-1), ones(H_B),
        qg[H_B:].reshape(-1), kg[H_B:].reshape(-1), ones(H_C)])[None, :]
    return w, colgain


def _layer(x, g_mix, w_in, q_gain, k_gain, w_branch, w_out, g_ffn, w_gu, w_down, tab_b, tabs_c):
    w, colgain = _proj_weights(w_in, q_gain, k_gain)
    qkv, d1, d4, d16, gates = _proj(x, g_mix[None, :], w, colgain, tm=1024)

    oa = _sb_attention(qkv, blk=256)
    ob = _moba_attention(qkv, tab_b)
    oc = _dilated_attention(d1, d4, d16, tabs_c)

    x = _merge_out(oa, ob, oc, gates, w_branch.astype(BF16), w_out.astype(BF16), x, tm=256)
    return _ffn(x, g_ffn[None, :], w_gu.astype(BF16), w_down.astype(BF16), tm=1024, tf=512)


def kernel(x, g_mix, w_in, q_gain, k_gain, w_branch, w_out, g_ffn, w_gu, w_down, rel_bias):
    depth = g_mix.shape[0]
    tab_b, tabs_c = _bias_tables(rel_bias, x.shape[1])
    for l in range(depth):
        x = _layer(x, g_mix[l], w_in[l], q_gain[l], k_gain[l], w_branch[l], w_out[l],
                   g_ffn[l], w_gu[l], w_down[l], tab_b, tabs_c)
    return x
```

```python
import functools
import math

import jax
import jax.numpy as jnp
from jax import lax
from jax.experimental import pallas as pl
from jax.experimental.pallas import tpu as pltpu

D_MODEL = 2048
HEAD_DIM = 128
H_A = 4
H_B = 6
H_C = 6
N_HEADS = H_A + H_B + H_C
MIX_WIDTH = N_HEADS * HEAD_DIM
N_BRANCH = 3
D_FF = 5632
MOBA_BLOCK = 256
MOBA_TOPK = 3
DILATIONS = ((128, 1), (512, 4), (2048, 16))
BAND_BLOCK = 128
N_BUCKETS = 32
MAX_DISTANCE = 2048
RMS_EPS = 1e-6
NEG_INF = -1e30
LOG2E = math.log2(math.e)

LANES = 128
VMEM_LIMIT = 56 * 1024 * 1024

SB_SKIP_SUM = 110.0

F32 = jnp.float32
BF16 = jnp.bfloat16

SLOTS_MAIN = 3 * (H_A + H_B)
SLOTS_DIL = 3 * H_C
SB_Q, SB_K, SB_V = 0, H_A, 2 * H_A
MB_Q, MB_K, MB_V = 3 * H_A, 3 * H_A + H_B, 3 * H_A + 2 * H_B


def _params(sem):
    return pltpu.CompilerParams(dimension_semantics=sem, vmem_limit_bytes=VMEM_LIMIT)


def _dot(a, b):
    return jnp.dot(a, b, preferred_element_type=F32)


def _dot_nt(a, b):
    return lax.dot_general(a, b, (((1,), (1,)), ((), ())), preferred_element_type=F32)


def _rows(n, size, length=None):
    start = n * size if isinstance(n, int) else pl.multiple_of(n * size, size)
    return pl.ds(start, size if length is None else length)


def _rms_rows(x):
    return x * lax.rsqrt(jnp.mean(x * x, axis=-1, keepdims=True) + RMS_EPS)


PROJ_HEADS = 6
PROJ_TN = PROJ_HEADS * LANES
J_MAIN = SLOTS_MAIN // PROJ_HEADS
J_DIL = SLOTS_DIL // PROJ_HEADS
J_GATE = N_BRANCH * D_MODEL // PROJ_TN
NORMED_TILES = (2, 3, J_MAIN, J_MAIN + 1)
PROJ_SUB = 256
PROJ_PIECE = 2 * LANES
PROJ_PIECES_PER_TILE = PROJ_TN // PROJ_PIECE


def _proj_pieces():
    pairs = N_HEADS // 2
    block = lambda part, head: part * pairs + head // 2
    q, k, v = 0, 1, 2
    tiles = [
        [block(q, 0), block(q, 2), block(k, 0)], [block(k, 2), block(v, 0), block(v, 2)],
        *[[block(part, h) for h in range(H_A, H_A + H_B, 2)] for part in (q, k, v)],
        *[[block(part, h) for h in range(H_A + H_B, N_HEADS, 2)] for part in (q, k, v)],
    ]
    gates0 = 3 * pairs
    tiles += [[gates0 + PROJ_PIECES_PER_TILE * t + p for p in range(PROJ_PIECES_PER_TILE)] for t in range(J_GATE)]
    return [b for tile in tiles for b in tile]


def _proj_kernel(src_ref, x_ref, g_ref, w0_ref, w1_ref, w2_ref, cg_ref, main_ref, d1_ref, d4_ref, d16_ref, gates_ref,
                 h_ref, y_ref, y4_ref):
    del src_ref
    j = pl.program_id(2)
    tm = h_ref.shape[0]
    sub = PROJ_SUB

    @pl.when(j == 0)
    def _():
        h_ref[...] = (_rms_rows(x_ref[0]) * g_ref[...]).astype(BF16)

    is_gate = j >= J_MAIN + J_DIL
    is_dil = (j >= J_MAIN) & jnp.logical_not(is_gate)
    is_norm = functools.reduce(jnp.logical_or, [j == t for t in NORMED_TILES])

    def sweep(epilogue):
        for mi in range(tm // sub):
            rows = slice(mi * sub, (mi + 1) * sub)
            h = h_ref[rows, :]
            epilogue(mi, rows, jnp.concatenate([_dot(h, w[...]) for w in (w0_ref, w1_ref, w2_ref)], axis=1))

    def heads(acc, normed):
        for hh in range(PROJ_HEADS):
            sl = slice(hh * LANES, (hh + 1) * LANES)
            yield hh, (_rms_rows(acc[:, sl]) if normed else acc[:, sl]) * cg_ref[:, sl]

    @pl.when(is_gate)
    def _():
        def epilogue(mi, rows, acc):
            gates_ref[0, rows, :] = jax.nn.sigmoid(acc).astype(BF16)
        sweep(epilogue)

    for normed in (False, True):
        @pl.when((j < J_MAIN) & (is_norm == normed))
        def _():
            def epilogue(mi, rows, acc):
                for hh, y in heads(acc, normed):
                    main_ref[0, hh, rows, :] = y.astype(BF16)
            sweep(epilogue)

        @pl.when(is_dil & (is_norm == normed))
        def _():
            def epilogue(mi, rows, acc):
                for hh, y in heads(acc, normed):
                    d1_ref[0, hh, 0, rows, :] = y.astype(BF16)
                    y_ref[mi % 2, hh] = y
                    for c in range(4):
                        y4 = y_ref[mi % 2, hh, pl.ds(c, sub // 4, stride=4), :]
                        d4_ref[0, hh, c, mi * (sub // 4):(mi + 1) * (sub // 4), :] = y4.astype(BF16)
                        y4_ref[mi % 2, hh, c] = y4
                        for k in range(4):
                            d16_ref[0, hh, c + 4 * k, mi * (sub // 16):(mi + 1) * (sub // 16), :] = (
                                y4_ref[mi % 2, hh, c, pl.ds(k, sub // 16, stride=4), :].astype(BF16))
            sweep(epilogue)


def _proj(x, g, w_in, colgain, *, tm):
    B, S, D = x.shape
    tn, hp, npc = PROJ_TN, PROJ_HEADS, PROJ_PIECES_PER_TILE
    jd, jg = J_MAIN, J_MAIN + J_DIL
    piece = lambda p: pl.BlockSpec((D, PROJ_PIECE), lambda b, s, j, src: (0, src[npc * j + p]))
    dil = lambda r: pl.BlockSpec((1, hp, r, tm // r, LANES),
                                 lambda b, s, j, src: (b, jnp.clip(j - jd, 0, J_DIL - 1), 0, s, 0))
    dil_shape = lambda r: jax.ShapeDtypeStruct((B, SLOTS_DIL, r, S // r, LANES), BF16)
    grid_spec = pltpu.PrefetchScalarGridSpec(
        num_scalar_prefetch=1,
        grid=(B, S // tm, J_MAIN + J_DIL + J_GATE),
        in_specs=[
            pl.BlockSpec((1, tm, D), lambda b, s, j, src: (b, s, 0)),
            pl.BlockSpec((1, D), lambda b, s, j, src: (0, 0)),
            piece(0), piece(1), piece(2),
            pl.BlockSpec((1, tn), lambda b, s, j, src: (0, jnp.minimum(j, jg - 1))),
        ],
        out_specs=[
            pl.BlockSpec((1, hp, tm, LANES), lambda b, s, j, src: (b, jnp.minimum(j, jd - 1), s, 0)),
            dil(1), dil(4), dil(16),
            pl.BlockSpec((1, tm, tn), lambda b, s, j, src: (b, s, jnp.maximum(j - jg, 0))),
        ],
        scratch_shapes=[pltpu.VMEM((tm, D), BF16), pltpu.VMEM((2, hp, PROJ_SUB, LANES), F32),
                        pltpu.VMEM((2, hp, 4, PROJ_SUB // 4, LANES), F32)],
    )
    return pl.pallas_call(
        _proj_kernel,
        grid_spec=grid_spec,
        out_shape=[
            jax.ShapeDtypeStruct((B, SLOTS_MAIN, S, LANES), BF16),
            dil_shape(1), dil_shape(4), dil_shape(16),
            jax.ShapeDtypeStruct((B, S, N_BRANCH * D_MODEL), BF16),
        ],
        compiler_params=_params(("arbitrary", "arbitrary", "arbitrary")),
        name="proj",
    )(jnp.asarray(_proj_pieces(), jnp.int32), x, g, w_in, w_in, w_in, colgain)


def _softplus2(z):
    return jnp.maximum(z, 0.0) + jnp.log2(1.0 + jnp.exp2(-jnp.abs(z)))


def _sb_kernel(q_ref, k_ref, v_ref, u_ref, o_ref, *, blk):
    i = pl.program_id(1)
    u = u_ref[...]
    row = lax.broadcasted_iota(jnp.int32, (blk, blk), 0)
    col = lax.broadcasted_iota(jnp.int32, (blk, blk), 1)
    past = col < row

    def blocks(j, state, diag):
        keys = pl.ds(pl.multiple_of(j * blk, blk), blk)
        heads = range(H_A)
        z = [_dot_nt(q_ref[0, h], k_ref[0, h, keys, :]) for h in heads]
        sp = [_softplus2(z[h]) for h in heads]
        spm = [jnp.where(past, sp[h], 0.0) for h in heads] if diag else sp
        hi = [spm[h].astype(BF16) for h in heads]
        lo = [(spm[h] - hi[h].astype(F32)).astype(BF16) for h in heads]
        after = [_dot(hi[h], u) + _dot(lo[h], u) for h in heads]
        w = [jnp.exp2(z[h] - sp[h] - after[h] - state[h][0]) for h in heads]
        if diag:
            w = [jnp.where(past, w[h], 0.0) for h in heads]
        acc = [state[h][1] + _dot(w[h].astype(BF16), v_ref[0, h, keys, :]) for h in heads]
        c = [state[h][0] + after[h][:, :1] + spm[h][:, :1] for h in heads]
        return tuple(zip(c, acc))

    def smallest(state):
        return functools.reduce(jnp.minimum, [jnp.min(c) for c, _ in state])

    zeros = (jnp.zeros((blk, 1), F32), jnp.zeros((blk, HEAD_DIM), F32))
    state = blocks(i, (zeros,) * H_A, True)

    def cond(carry):
        j, c_min, _ = carry
        return (j >= 0) & (c_min < SB_SKIP_SUM * LOG2E)

    def body(carry):
        j, _, state = carry
        state = blocks(j, state, False)
        return j - 1, smallest(state), state

    _, _, state = lax.while_loop(cond, body, (i - 1, smallest(state), state))
    for h, (_, acc) in enumerate(state):
        o_ref[0, :, h * LANES:(h + 1) * LANES] = acc.astype(BF16)


def _sb_attention(qkv, *, blk):
    B, _, S, _ = qkv.shape
    tri = (jnp.arange(blk)[:, None] > jnp.arange(blk)[None, :]).astype(BF16)
    return pl.pallas_call(
        functools.partial(_sb_kernel, blk=blk),
        grid=(B, S // blk),
        in_specs=[
            pl.BlockSpec((1, H_A, blk, LANES), lambda b, i: (b, SB_Q // H_A, i, 0)),
            pl.BlockSpec((1, H_A, S, LANES), lambda b, i: (b, SB_K // H_A, 0, 0)),
            pl.BlockSpec((1, H_A, S, LANES), lambda b, i: (b, SB_V // H_A, 0, 0)),
            pl.BlockSpec((blk, blk), lambda b, i: (0, 0)),
        ],
        out_specs=pl.BlockSpec((1, blk, H_A * LANES), lambda b, i: (b, i, 0)),
        out_shape=jax.ShapeDtypeStruct((B, S, H_A * HEAD_DIM), BF16),
        compiler_params=_params(("arbitrary", "arbitrary")),
        name="sb_attention",
    )(qkv, qkv, qkv, tri)


def _toeplitz(row_vals, n_rows, shift):
    full = jnp.broadcast_to(row_vals, (n_rows, row_vals.shape[1]))
    return pltpu.roll(full, shift, 1, stride=1, stride_axis=0)


MOBA_STRIP = 32
GATE_UNROLL = 4


def _moba_kernel(q_ref, k_ref, v_ref, tab_ref, o_ref, tb_ref, ob_ref, ka_ref, va_ref, qa_ref, m_ref, accl_ref, sc_ref,
                 p_ref, al_ref, *, nblk, n_sel):
    blk = MOBA_BLOCK
    wide = 2 * blk
    npair = nblk // 2

    row = lax.broadcasted_iota(jnp.int32, (blk, blk), 0)
    col = lax.broadcasted_iota(jnp.int32, (blk, blk), 1)

    @pl.when(pl.program_id(1) == 0)
    def _():
        for d in range(nblk):
            tb_ref[d] = _toeplitz(tab_ref[0, d:d + 1, :], blk, blk + 1)[:, :blk]
        diag = jnp.where(col <= row, tb_ref[0], NEG_INF)
        ob_ref[:blk, :blk] = diag
        ob_ref[:blk, blk:] = jnp.zeros((blk, blk), F32)
        ob_ref[blk:, :blk] = tb_ref[1]
        ob_ref[blk:, blk:] = diag

    lane = lax.broadcasted_iota(jnp.int32, (blk, LANES), 1)
    va_ref[:, LANES:] = jnp.ones((va_ref.shape[0], LANES), BF16)
    means = []
    for n in range(nblk):
        rows = slice(n * blk, (n + 1) * blk)
        kb = k_ref[0, 0, rows, :]
        means.append(jnp.mean(kb.astype(F32), axis=0, keepdims=True))
        ka_ref[rows, :LANES] = kb
        ka_ref[rows, LANES:] = jnp.where(lane == n, 1.0, 0.0).astype(BF16)
        va_ref[rows, :LANES] = v_ref[0, 0, rows, :]
    km = jnp.concatenate(means, axis=0).astype(BF16)

    blk_id = lax.broadcasted_iota(jnp.int32, (nblk, blk), 0)
    eye = jnp.where(row == col, 1.0, 0.0).astype(BF16)

    def gating(g, _):
        ids = [g * GATE_UNROLL + k for k in range(GATE_UNROLL)]
        windows = [pl.ds(pl.multiple_of(g * (GATE_UNROLL * blk), GATE_UNROLL * blk) + k * blk, blk)
                   for k in range(GATE_UNROLL)]
        q = [q_ref[0, 0, rows, :] for rows in windows]
        fully_past = [blk_id < i for i in ids]
        gate = [jnp.where(fp, _dot_nt(km, x), NEG_INF) for fp, x in zip(fully_past, q)]
        sel = [jnp.zeros((nblk, blk), F32) for _ in ids]
        for _ in range(n_sel):
            top = [jnp.max(x, axis=0, keepdims=True) for x in gate]
            first = [jnp.min(jnp.where(x == t, blk_id, nblk), axis=0, keepdims=True)
                     for x, t in zip(gate, top)]
            pick = [blk_id == f for f in first]
            sel = [jnp.where(pk, 1.0, x) for pk, x in zip(pick, sel)]
            gate = [jnp.where(pk, -jnp.inf, x) for pk, x in zip(pick, gate)]
        pad = jnp.zeros((LANES - nblk, blk), BF16)
        pen_t = [jnp.concatenate([jnp.where((blk_id == i) | (fp & (x > 0.0)), 0.0, NEG_INF).astype(BF16), pad], axis=0)
                 for i, fp, x in zip(ids, fully_past, sel)]
        pen = [_dot_nt(eye, x).astype(BF16) for x in pen_t]
        for rows, x, y in zip(windows, q, pen):
            qa_ref[rows, :LANES] = x
            qa_ref[rows, LANES:] = y
        return 0

    lax.fori_loop(0, nblk // GATE_UNROLL, gating, 0)


    def aligned(x, m):
        return x if isinstance(x, int) else pl.multiple_of(x, m)

    def pair_bias(d):
        return jnp.concatenate([jnp.concatenate([tb_ref[d], tb_ref[d - 1]], axis=1),
                                jnp.concatenate([tb_ref[d + 1], tb_ref[d]], axis=1)], axis=0)

    def soften(t, j, slot):
        s_base = aligned(j * wide, wide)
        q_base = s_base if t is None else aligned((t + 1 + j) * wide, wide)
        for st in range(wide // MOBA_STRIP):
            off = st * MOBA_STRIP
            s = sc_ref[pl.ds(s_base + off, MOBA_STRIP), :]
            state = pl.ds(q_base + off, MOBA_STRIP)
            m = jnp.broadcast_to(jnp.max(s, axis=-1, keepdims=True), (MOBA_STRIP, LANES))
            if t is not None:
                m_old = m_ref[state, :]
                m = jnp.maximum(m_old, m)
                al_ref[slot, off:off + MOBA_STRIP, :] = jnp.exp2(m_old - m)
            p_ref[slot, off:off + MOBA_STRIP, :] = jnp.exp2(s - jnp.concatenate([m] * (wide // LANES), axis=1)).astype(BF16)
            m_ref[state, :] = m

    def accumulate(t, j, slot):
        if t is None:
            rows = _rows(j, wide)
            accl_ref[rows, :] = _dot(p_ref[slot], va_ref[rows, :])
        else:
            rows = _rows(t + 1 + j, wide)
            al = al_ref[slot]
            accl_ref[rows, :] = (jnp.concatenate([al, al], axis=1) * accl_ref[rows, :]
                                 + _dot(p_ref[slot], va_ref[t * wide:(t + 1) * wide, :]))

    def phase(t, n):
        soften(t, 0, 0)

        def body(j, _):
            accumulate(t, j - 1, (j - 1) % 2)
            soften(t, j, j % 2)
            return 0

        lax.fori_loop(1, n, body, 0)
        accumulate(t, n - 1, (n - 1) % 2)

    for u in range(npair):
        rows = slice(u * wide, (u + 1) * wide)
        sc_ref[rows, :] = _dot_nt(qa_ref[rows, :], ka_ref[rows, :]) + ob_ref[...]
    phase(None, npair)
    for t in range(npair - 1):
        n = npair - 1 - t
        for j in range(n):
            rows = slice((t + 1 + j) * wide, (t + 2 + j) * wide)
            sc_ref[j * wide:(j + 1) * wide, :] = (_dot_nt(qa_ref[rows, :], ka_ref[t * wide:(t + 1) * wide, :])
                                                  + pair_bias(2 + 2 * j))
        phase(t, n)


    def finish(i, _):
        rows = _rows(i, blk)
        o_ref[0, rows, :] = (accl_ref[rows, :LANES] / accl_ref[rows, LANES:]).astype(BF16)
        return 0

    lax.fori_loop(0, nblk, finish, 0)


def _moba_attention(qkv, tab):
    B, _, S, _ = qkv.shape
    blk = MOBA_BLOCK
    nblk = S // blk
    assert nblk % 2 == 0 and nblk % GATE_UNROLL == 0 and nblk <= LANES
    n_sel = min(MOBA_TOPK, nblk - 1)
    head = lambda slot: pl.BlockSpec((1, 1, S, LANES), lambda h, b: (b, slot + h, 0, 0))
    return pl.pallas_call(
        functools.partial(_moba_kernel, nblk=nblk, n_sel=n_sel),
        grid=(H_B, B),
        in_specs=[head(MB_Q), head(MB_K), head(MB_V), pl.BlockSpec((1, nblk, 2 * blk), lambda h, b: (h, 0, 0))],
        out_specs=pl.BlockSpec((1, S, LANES), lambda h, b: (b, 0, h)),
        out_shape=jax.ShapeDtypeStruct((B, S, H_B * HEAD_DIM), BF16),
        scratch_shapes=[pltpu.VMEM((nblk, blk, blk), F32), pltpu.VMEM((2 * blk, 2 * blk), F32),
                        pltpu.VMEM((S, 2 * LANES), BF16), pltpu.VMEM((S, 2 * LANES), BF16),
                        pltpu.VMEM((S, 2 * LANES), BF16), pltpu.VMEM((S, LANES), F32),
                        pltpu.VMEM((S, 2 * LANES), F32), pltpu.VMEM((S, 2 * blk), F32),
                        pltpu.VMEM((2, 2 * blk, 2 * blk), BF16), pltpu.VMEM((2, 2 * blk, LANES), F32)],
        compiler_params=_params(("arbitrary", "arbitrary")),
        name="moba_attention",
    )(qkv, qkv, qkv, tab)


DIL_UNROLL = 8


def _dilated_kernel(q1_ref, k1_ref, v1_ref, q4_ref, k4_ref, v4_ref, q16_ref, k16_ref, v16_ref, tab_ref, o_ref,
                    c4o_ref, c4l_ref, c16o_ref, c16l_ref, s4o_ref, s4l_ref, s16o_ref, s16l_ref, *, seq):
    bb = BAND_BLOCK
    qi = lax.broadcasted_iota(jnp.int32, (bb, 2 * bb), 0)
    kj = lax.broadcasted_iota(jnp.int32, (bb, 2 * bb), 1)
    visible = (kj >= qi) & (kj <= qi + bb)

    def pattern(g, r, q_ref, k_ref, v_ref, emit):
        bias = jnp.where(visible, _toeplitz(tab_ref[0, g:g + 1, :], bb, 0), NEG_INF)
        bias_first = jnp.concatenate([bias[:, bb:], jnp.full((bb, bb), NEG_INF, F32)], axis=1)
        nb = seq // r // bb

        def body(t, _):
            blocks = [((t * DIL_UNROLL + k) // nb, (t * DIL_UNROLL + k) % nb) for k in range(DIL_UNROLL)]
            keys = [_rows(jnp.maximum(n - 1, 0), bb, 2 * bb) for _, n in blocks]
            s = [_dot_nt(q_ref[0, 0, c, _rows(n, bb), :], k_ref[0, 0, c, kw, :]) + jnp.where(n == 0, bias_first, bias)
                 for (c, n), kw in zip(blocks, keys)]
            m = [jnp.max(x, axis=-1, keepdims=True) for x in s]
            p = [jnp.exp2(x - mx) for x, mx in zip(s, m)]
            den = [jnp.sum(x, axis=-1, keepdims=True) for x in p]
            o = [_dot(x.astype(BF16), v_ref[0, 0, c, kw, :]) for x, (c, _), kw in zip(p, blocks, keys)]
            for (c, n), ox, dx, mx in zip(blocks, o, den, m):
                emit(c, n, ox / dx, mx + jnp.log2(dx))
            return 0

        lax.fori_loop(0, r * nb // DIL_UNROLL, body, 0)

    def class_major(o_cm, l_cm):
        def emit(c, n, o, lse):
            o_cm[c, _rows(n, bb), :] = o
            l_cm[c, _rows(n, bb), :] = jnp.broadcast_to(lse, (bb, LANES))
        return emit

    def to_sequence(cm4, sq):
        for c in range(4):
            sq[pl.ds(c, seq // 4, stride=4), :] = cm4[c]

    def to_classes_of_4(cm16, cm4):
        for c in range(4):
            for k in range(4):
                cm4[c, pl.ds(k, seq // 16, stride=4), :] = cm16[c + 4 * k]

    pattern(2, 16, q16_ref, k16_ref, v16_ref, class_major(c16o_ref, c16l_ref))
    for cm16, sq in ((c16o_ref, s16o_ref), (c16l_ref, s16l_ref)):
        to_classes_of_4(cm16, c4o_ref)
        to_sequence(c4o_ref, sq)
    pattern(1, 4, q4_ref, k4_ref, v4_ref, class_major(c4o_ref, c4l_ref))
    to_sequence(c4o_ref, s4o_ref)
    to_sequence(c4l_ref, s4l_ref)

    def merge(c, n, o1, l1):
        rows = _rows(n, bb)
        l4, l16 = s4l_ref[rows, :], s16l_ref[rows, :]
        top = jnp.maximum(jnp.maximum(l1, l4), l16)
        e1, e4, e16 = jnp.exp2(l1 - top), jnp.exp2(l4 - top), jnp.exp2(l16 - top)
        mix = e1 * o1 + e4 * s4o_ref[rows, :] + e16 * s16o_ref[rows, :]
        o_ref[0, rows, :] = (mix / (e1 + e4 + e16)).astype(BF16)

    pattern(0, 1, q1_ref, k1_ref, v1_ref, merge)


def _dilated_attention(d1, d4, d16, tabs):
    B, S = d1.shape[0], d1.shape[3]
    assert [r for _, r in DILATIONS] == [1, 4, 16] and all(w // r == BAND_BLOCK for w, r in DILATIONS)
    assert (S // BAND_BLOCK) % DIL_UNROLL == 0
    head = lambda r, part: pl.BlockSpec((1, 1, r, S // r, LANES), lambda b, h: (b, part * H_C + h, 0, 0, 0))
    class_major = lambda r: pltpu.VMEM((r, S // r, LANES), F32)
    return pl.pallas_call(
        functools.partial(_dilated_kernel, seq=S),
        grid=(B, H_C),
        in_specs=[head(r, part) for r in (1, 4, 16) for part in range(3)]
        + [pl.BlockSpec((1, len(DILATIONS), 2 * BAND_BLOCK), lambda b, h: (h, 0, 0))],
        out_specs=pl.BlockSpec((1, S, LANES), lambda b, h: (b, 0, h)),
        out_shape=jax.ShapeDtypeStruct((B, S, H_C * HEAD_DIM), BF16),
        scratch_shapes=[class_major(4), class_major(4), class_major(16), class_major(16)]
        + [pltpu.VMEM((S, LANES), F32)] * 4,
        compiler_params=_params(("arbitrary", "arbitrary")),
        name="dilated_attention",
    )(d1, d1, d1, d4, d4, d4, d16, d16, d16, tabs)


def _merge_kernel(oa_ref, ob_ref, oc_ref, g0_ref, g1_ref, g2_ref, wbr_ref, wo_ref, x_ref, out_ref):
    ea, eb = H_A * HEAD_DIM, (H_A + H_B) * HEAD_DIM
    merged = (g0_ref[0].astype(F32) * _dot(oa_ref[0], wbr_ref[:ea, :])
              + g1_ref[0].astype(F32) * _dot(ob_ref[0], wbr_ref[ea:eb, :])
              + g2_ref[0].astype(F32) * _dot(oc_ref[0], wbr_ref[eb:, :]))
    out_ref[0] = x_ref[0] + _dot(merged.astype(BF16), wo_ref[...])


def _merge_out(oa, ob, oc, gates, w_branch, wo, x, *, tm):
    B, S, D = x.shape
    rows = lambda w: pl.BlockSpec((1, tm, w), lambda b, s: (b, s, 0))
    gate = lambda g: pl.BlockSpec((1, tm, D), lambda b, s: (b, s, g))
    resident = lambda w: pl.BlockSpec(w.shape, lambda b, s: (0, 0), pipeline_mode=pl.Buffered(1))
    return pl.pallas_call(
        _merge_kernel,
        grid=(B, S // tm),
        in_specs=[rows(oa.shape[-1]), rows(ob.shape[-1]), rows(oc.shape[-1]), gate(0), gate(1), gate(2),
                  resident(w_branch), resident(wo), rows(D)],
        out_specs=rows(D),
        out_shape=jax.ShapeDtypeStruct((B, S, D), F32),
        compiler_params=_params(("arbitrary", "arbitrary")),
        name="merge_out",
    )(oa, ob, oc, gates, gates, gates, w_branch, wo, x)


FFN_SUB = 256


def _ffn_kernel(x_ref, g_ref, wg_ref, wu_ref, wd_ref, o_ref, h_ref):
    @pl.when(pl.program_id(2) == 0)
    def _():
        x = x_ref[0]
        h_ref[...] = (_rms_rows(x) * g_ref[...]).astype(BF16)
        o_ref[0] = x

    for mi in range(h_ref.shape[0] // FFN_SUB):
        rows = slice(mi * FFN_SUB, (mi + 1) * FFN_SUB)
        h = h_ref[rows, :]
        gate = _dot(h, wg_ref[...])
        up = _dot(h, wu_ref[...])
        act = (gate * jax.nn.sigmoid(gate) * up).astype(BF16)
        o_ref[0, rows, :] += _dot(act, wd_ref[...])


def _ffn(x, g, w_gu, w_down, *, tm, tf):
    B, S, D = x.shape
    nf = D_FF // tf
    return pl.pallas_call(
        _ffn_kernel,
        grid=(B, S // tm, nf),
        in_specs=[
            pl.BlockSpec((1, tm, D), lambda b, s, f: (b, s, 0)),
            pl.BlockSpec((1, D), lambda b, s, f: (0, 0)),
            pl.BlockSpec((D, tf), lambda b, s, f: (0, f)),
            pl.BlockSpec((D, tf), lambda b, s, f: (0, nf + f)),
            pl.BlockSpec((tf, D), lambda b, s, f: (f, 0)),
        ],
        out_specs=pl.BlockSpec((1, tm, D), lambda b, s, f: (b, s, 0)),
        out_shape=jax.ShapeDtypeStruct((B, S, D), F32),
        scratch_shapes=[pltpu.VMEM((tm, D), BF16)],
        compiler_params=_params(("arbitrary", "arbitrary", "arbitrary")),
        name="ffn",
    )(x, g, w_gu, w_gu, w_down)


def _t5_bucket(dist):
    max_exact = N_BUCKETS // 2
    d = jnp.maximum(dist, 0)
    df = jnp.maximum(d, 1).astype(F32)
    large = max_exact + (jnp.log(df / max_exact) / math.log(MAX_DISTANCE / max_exact)
                         * (N_BUCKETS - max_exact)).astype(jnp.int32)
    large = jnp.minimum(large, N_BUCKETS - 1)
    return jnp.where(d < max_exact, d, large)


def _bias_tables(rel_bias, S):
    by_dist = rel_bias[_t5_bucket(jnp.arange(S))].T.astype(F32) * LOG2E
    nblk, blk, bb = S // MOBA_BLOCK, MOBA_BLOCK, BAND_BLOCK
    rev = jnp.pad(by_dist[:H_B], ((0, 0), (blk, 0)))[:, ::-1]
    tab_b = jnp.stack([rev[:, S - blk * (d + 1): S - blk * (d - 1)] for d in range(nblk)], axis=1)
    tabs_c = []
    for _, r in DILATIONS:
        near = by_dist[H_B:, ::r][:, :bb + 1]
        tabs_c.append(jnp.pad(near[:, ::-1], ((0, 0), (0, bb - 1))))
    return tab_b, jnp.stack(tabs_c, axis=1)


def _proj_colgain(q_gain, k_gain):
    scale = HEAD_DIM ** -0.5
    ones = lambda n: jnp.ones((n * HEAD_DIM,), F32)
    qg, kg = q_gain.astype(F32) * (scale * LOG2E), k_gain.astype(F32)
    return jnp.concatenate([
        ones(H_A) * (scale * LOG2E), ones(H_A), ones(H_A),
        qg[:H_B].reshape(-1), kg[:H_B].reshape(-1), ones(H_B),
        qg[H_B:].reshape(-1), kg[H_B:].reshape(-1), ones(H_C)])[None, :]


def _layer(x, g_mix, w_in, q_gain, k_gain, w_branch, w_out, g_ffn, w_gu, w_down, tab_b, tabs_c):
    qkv, d1, d4, d16, gates = _proj(x, g_mix[None, :], w_in.astype(BF16), _proj_colgain(q_gain, k_gain), tm=1024)

    oa = _sb_attention(qkv, blk=256)
    ob = _moba_attention(qkv, tab_b)
    oc = _dilated_attention(d1, d4, d16, tabs_c)

    x = _merge_out(oa, ob, oc, gates, w_branch.astype(BF16), w_out.astype(BF16), x, tm=256)
    return _ffn(x, g_ffn[None, :], w_gu.astype(BF16), w_down.astype(BF16), tm=1024, tf=512)


def kernel(x, g_mix, w_in, q_gain, k_gain, w_branch, w_out, g_ffn, w_gu, w_down, rel_bias):
    depth = g_mix.shape[0]
    tab_b, tabs_c = _bias_tables(rel_bias, x.shape[1])
    for l in range(depth):
        x = _layer(x, g_mix[l], w_in[l], q_gain[l], k_gain[l], w_branch[l], w_out[l],
                   g_ffn[l], w_gu[l], w_down[l], tab_b, tabs_c)
    return x
```

```python
import functools
import math

import jax
import jax.numpy as jnp
from jax import lax
from jax.experimental import pallas as pl
from jax.experimental.pallas import tpu as pltpu

D_MODEL = 2048
HEAD_DIM = 128
H_A = 4
H_B = 6
H_C = 6
N_HEADS = H_A + H_B + H_C
MIX_WIDTH = N_HEADS * HEAD_DIM
N_BRANCH = 3
D_FF = 5632
MOBA_BLOCK = 256
MOBA_TOPK = 3
DILATIONS = ((128, 1), (512, 4), (2048, 16))
BAND_BLOCK = 128
N_BUCKETS = 32
MAX_DISTANCE = 2048
RMS_EPS = 1e-6
NEG_INF = -1e30
LOG2E = math.log2(math.e)

LANES = 128
VMEM_LIMIT = 56 * 1024 * 1024

SB_SKIP_SUM = 110.0

F32 = jnp.float32
BF16 = jnp.bfloat16

SLOTS_MAIN = 3 * (H_A + H_B)
SLOTS_DIL = 3 * H_C
SB_Q, SB_K, SB_V = 0, H_A, 2 * H_A
MB_Q, MB_K, MB_V = 3 * H_A, 3 * H_A + H_B, 3 * H_A + 2 * H_B


def _params(sem):
    return pltpu.CompilerParams(dimension_semantics=sem, vmem_limit_bytes=VMEM_LIMIT)


def _dot(a, b):
    return jnp.dot(a, b, preferred_element_type=F32)


def _dot_nt(a, b):
    return lax.dot_general(a, b, (((1,), (1,)), ((), ())), preferred_element_type=F32)


def _rows(n, size, length=None):
    start = n * size if isinstance(n, int) else pl.multiple_of(n * size, size)
    return pl.ds(start, size if length is None else length)


def _rms_rows(x):
    return x * lax.rsqrt(jnp.mean(x * x, axis=-1, keepdims=True) + RMS_EPS)


PROJ_HEADS = 6
PROJ_TN = PROJ_HEADS * LANES
J_MAIN = SLOTS_MAIN // PROJ_HEADS
J_DIL = SLOTS_DIL // PROJ_HEADS
J_GATE = N_BRANCH * D_MODEL // PROJ_TN
NORMED_TILES = (2, 3, J_MAIN, J_MAIN + 1)
PROJ_SUB = 256
PROJ_PIECE = 2 * LANES
PROJ_PIECES_PER_TILE = PROJ_TN // PROJ_PIECE


def _proj_pieces():
    pairs = N_HEADS // 2
    block = lambda part, head: part * pairs + head // 2
    q, k, v = 0, 1, 2
    tiles = [
        [block(q, 0), block(q, 2), block(k, 0)], [block(k, 2), block(v, 0), block(v, 2)],
        *[[block(part, h) for h in range(H_A, H_A + H_B, 2)] for part in (q, k, v)],
        *[[block(part, h) for h in range(H_A + H_B, N_HEADS, 2)] for part in (q, k, v)],
    ]
    gates0 = 3 * pairs
    tiles += [[gates0 + PROJ_PIECES_PER_TILE * t + p for p in range(PROJ_PIECES_PER_TILE)] for t in range(J_GATE)]
    return [b for tile in tiles for b in tile]


def _proj_kernel(src_ref, x_ref, g_ref, w0_ref, w1_ref, w2_ref, cg_ref, main_ref, d1_ref, d4_ref, d16_ref, gates_ref,
                 h_ref, y_ref, y4_ref):
    del src_ref
    j = pl.program_id(2)
    tm = h_ref.shape[0]
    sub = PROJ_SUB

    @pl.when(j == 0)
    def _():
        h_ref[...] = (_rms_rows(x_ref[0]) * g_ref[...]).astype(BF16)

    is_gate = j >= J_MAIN + J_DIL
    is_dil = (j >= J_MAIN) & jnp.logical_not(is_gate)
    is_norm = functools.reduce(jnp.logical_or, [j == t for t in NORMED_TILES])

    def sweep(epilogue):
        for mi in range(tm // sub):
            rows = slice(mi * sub, (mi + 1) * sub)
            h = h_ref[rows, :]
            epilogue(mi, rows, jnp.concatenate([_dot(h, w[...]) for w in (w0_ref, w1_ref, w2_ref)], axis=1))

    def heads(acc, normed):
        for hh in range(PROJ_HEADS):
            sl = slice(hh * LANES, (hh + 1) * LANES)
            yield hh, (_rms_rows(acc[:, sl]) if normed else acc[:, sl]) * cg_ref[:, sl]

    @pl.when(is_gate)
    def _():
        def epilogue(mi, rows, acc):
            gates_ref[0, rows, :] = jax.nn.sigmoid(acc).astype(BF16)
        sweep(epilogue)

    for normed in (False, True):
        @pl.when((j < J_MAIN) & (is_norm == normed))
        def _():
            def epilogue(mi, rows, acc):
                for hh, y in heads(acc, normed):
                    main_ref[0, hh, rows, :] = y.astype(BF16)
            sweep(epilogue)

        @pl.when(is_dil & (is_norm == normed))
        def _():
            def epilogue(mi, rows, acc):
                for hh, y in heads(acc, normed):
                    d1_ref[0, hh, 0, rows, :] = y.astype(BF16)
                    y_ref[mi % 2, hh] = y
                    for c in range(4):
                        y4 = y_ref[mi % 2, hh, pl.ds(c, sub // 4, stride=4), :]
                        d4_ref[0, hh, c, mi * (sub // 4):(mi + 1) * (sub // 4), :] = y4.astype(BF16)
                        y4_ref[mi % 2, hh, c] = y4
                        for k in range(4):
                            d16_ref[0, hh, c + 4 * k, mi * (sub // 16):(mi + 1) * (sub // 16), :] = (
                                y4_ref[mi % 2, hh, c, pl.ds(k, sub // 16, stride=4), :].astype(BF16))
            sweep(epilogue)


def _proj(x, g, w_in, layer, colgain, *, tm):
    B, S, D = x.shape
    tn, hp, npc = PROJ_TN, PROJ_HEADS, PROJ_PIECES_PER_TILE
    jd, jg = J_MAIN, J_MAIN + J_DIL
    piece = lambda p: pl.BlockSpec((None, D, PROJ_PIECE), lambda b, s, j, src: (layer, 0, src[npc * j + p]))
    dil = lambda r: pl.BlockSpec((1, hp, r, tm // r, LANES),
                                 lambda b, s, j, src: (b, jnp.clip(j - jd, 0, J_DIL - 1), 0, s, 0))
    dil_shape = lambda r: jax.ShapeDtypeStruct((B, SLOTS_DIL, r, S // r, LANES), BF16)
    grid_spec = pltpu.PrefetchScalarGridSpec(
        num_scalar_prefetch=1,
        grid=(B, S // tm, J_MAIN + J_DIL + J_GATE),
        in_specs=[
            pl.BlockSpec((1, tm, D), lambda b, s, j, src: (b, s, 0)),
            pl.BlockSpec((1, D), lambda b, s, j, src: (0, 0)),
            piece(0), piece(1), piece(2),
            pl.BlockSpec((1, tn), lambda b, s, j, src: (0, jnp.minimum(j, jg - 1))),
        ],
        out_specs=[
            pl.BlockSpec((1, hp, tm, LANES), lambda b, s, j, src: (b, jnp.minimum(j, jd - 1), s, 0)),
            dil(1), dil(4), dil(16),
            pl.BlockSpec((1, tm, tn), lambda b, s, j, src: (b, s, jnp.maximum(j - jg, 0))),
        ],
        scratch_shapes=[pltpu.VMEM((tm, D), BF16), pltpu.VMEM((2, hp, PROJ_SUB, LANES), F32),
                        pltpu.VMEM((2, hp, 4, PROJ_SUB // 4, LANES), F32)],
    )
    return pl.pallas_call(
        _proj_kernel,
        grid_spec=grid_spec,
        out_shape=[
            jax.ShapeDtypeStruct((B, SLOTS_MAIN, S, LANES), BF16),
            dil_shape(1), dil_shape(4), dil_shape(16),
            jax.ShapeDtypeStruct((B, S, N_BRANCH * D_MODEL), BF16),
        ],
        compiler_params=_params(("arbitrary", "arbitrary", "arbitrary")),
        name="proj",
    )(jnp.asarray(_proj_pieces(), jnp.int32), x, g, w_in, w_in, w_in, colgain)


def _softplus2(z):
    return jnp.maximum(z, 0.0) + jnp.log2(1.0 + jnp.exp2(-jnp.abs(z)))


def _sb_kernel(q_ref, k_ref, v_ref, u_ref, o_ref, *, blk):
    i = pl.program_id(1)
    u = u_ref[...]
    row = lax.broadcasted_iota(jnp.int32, (blk, blk), 0)
    col = lax.broadcasted_iota(jnp.int32, (blk, blk), 1)
    past = col < row

    def blocks(j, state, diag):
        keys = pl.ds(pl.multiple_of(j * blk, blk), blk)
        heads = range(H_A)
        z = [_dot_nt(q_ref[0, h], k_ref[0, h, keys, :]) for h in heads]
        sp = [_softplus2(z[h]) for h in heads]
        spm = [jnp.where(past, sp[h], 0.0) for h in heads] if diag else sp
        hi = [spm[h].astype(BF16) for h in heads]
        lo = [(spm[h] - hi[h].astype(F32)).astype(BF16) for h in heads]
        after = [_dot(hi[h], u) + _dot(lo[h], u) for h in heads]
        w = [jnp.exp2(z[h] - sp[h] - after[h] - state[h][0]) for h in heads]
        if diag:
            w = [jnp.where(past, w[h], 0.0) for h in heads]
        acc = [state[h][1] + _dot(w[h].astype(BF16), v_ref[0, h, keys, :]) for h in heads]
        c = [state[h][0] + after[h][:, :1] + spm[h][:, :1] for h in heads]
        return tuple(zip(c, acc))

    def smallest(state):
        return functools.reduce(jnp.minimum, [jnp.min(c) for c, _ in state])

    zeros = (jnp.zeros((blk, 1), F32), jnp.zeros((blk, HEAD_DIM), F32))
    state = blocks(i, (zeros,) * H_A, True)

    def cond(carry):
        j, c_min, _ = carry
        return (j >= 0) & (c_min < SB_SKIP_SUM * LOG2E)

    def body(carry):
        j, _, state = carry
        state = blocks(j, state, False)
        return j - 1, smallest(state), state

    _, _, state = lax.while_loop(cond, body, (i - 1, smallest(state), state))
    for h, (_, acc) in enumerate(state):
        o_ref[0, :, h * LANES:(h + 1) * LANES] = acc.astype(BF16)


def _sb_attention(qkv, *, blk):
    B, _, S, _ = qkv.shape
    tri = (jnp.arange(blk)[:, None] > jnp.arange(blk)[None, :]).astype(BF16)
    return pl.pallas_call(
        functools.partial(_sb_kernel, blk=blk),
        grid=(B, S // blk),
        in_specs=[
            pl.BlockSpec((1, H_A, blk, LANES), lambda b, i: (b, SB_Q // H_A, i, 0)),
            pl.BlockSpec((1, H_A, S, LANES), lambda b, i: (b, SB_K // H_A, 0, 0)),
            pl.BlockSpec((1, H_A, S, LANES), lambda b, i: (b, SB_V // H_A, 0, 0)),
            pl.BlockSpec((blk, blk), lambda b, i: (0, 0)),
        ],
        out_specs=pl.BlockSpec((1, blk, H_A * LANES), lambda b, i: (b, i, 0)),
        out_shape=jax.ShapeDtypeStruct((B, S, H_A * HEAD_DIM), BF16),
        compiler_params=_params(("arbitrary", "arbitrary")),
        name="sb_attention",
    )(qkv, qkv, qkv, tri)


def _toeplitz(row_vals, n_rows, shift):
    full = jnp.broadcast_to(row_vals, (n_rows, row_vals.shape[1]))
    return pltpu.roll(full, shift, 1, stride=1, stride_axis=0)


MOBA_STRIP = 32
GATE_UNROLL = 4


def _moba_kernel(q_ref, k_ref, v_ref, tab_ref, o_ref, tb_ref, ob_ref, ka_ref, va_ref, qa_ref, m_ref, accl_ref, sc_ref,
                 p_ref, al_ref, *, nblk, n_sel):
    blk = MOBA_BLOCK
    wide = 2 * blk
    npair = nblk // 2

    row = lax.broadcasted_iota(jnp.int32, (blk, blk), 0)
    col = lax.broadcasted_iota(jnp.int32, (blk, blk), 1)

    @pl.when(pl.program_id(1) == 0)
    def _():
        for d in range(nblk):
            tb_ref[d] = _toeplitz(tab_ref[0, d:d + 1, :], blk, blk + 1)[:, :blk]
        diag = jnp.where(col <= row, tb_ref[0], NEG_INF)
        ob_ref[:blk, :blk] = diag
        ob_ref[:blk, blk:] = jnp.zeros((blk, blk), F32)
        ob_ref[blk:, :blk] = tb_ref[1]
        ob_ref[blk:, blk:] = diag

    lane = lax.broadcasted_iota(jnp.int32, (blk, LANES), 1)
    va_ref[:, LANES:] = jnp.ones((va_ref.shape[0], LANES), BF16)
    means = []
    for n in range(nblk):
        rows = slice(n * blk, (n + 1) * blk)
        kb = k_ref[0, 0, rows, :]
        means.append(jnp.mean(kb.astype(F32), axis=0, keepdims=True))
        ka_ref[rows, :LANES] = kb
        ka_ref[rows, LANES:] = jnp.where(lane == n, 1.0, 0.0).astype(BF16)
        va_ref[rows, :LANES] = v_ref[0, 0, rows, :]
    km = jnp.concatenate(means, axis=0).astype(BF16)

    blk_id = lax.broadcasted_iota(jnp.int32, (nblk, blk), 0)
    eye = jnp.where(row == col, 1.0, 0.0).astype(BF16)

    def gating(g, _):
        ids = [g * GATE_UNROLL + k for k in range(GATE_UNROLL)]
        windows = [pl.ds(pl.multiple_of(g * (GATE_UNROLL * blk), GATE_UNROLL * blk) + k * blk, blk)
                   for k in range(GATE_UNROLL)]
        q = [q_ref[0, 0, rows, :] for rows in windows]
        fully_past = [blk_id < i for i in ids]
        gate = [jnp.where(fp, _dot_nt(km, x), NEG_INF) for fp, x in zip(fully_past, q)]
        sel = [jnp.zeros((nblk, blk), F32) for _ in ids]
        for _ in range(n_sel):
            top = [jnp.max(x, axis=0, keepdims=True) for x in gate]
            first = [jnp.min(jnp.where(x == t, blk_id, nblk), axis=0, keepdims=True)
                     for x, t in zip(gate, top)]
            pick = [blk_id == f for f in first]
            sel = [jnp.where(pk, 1.0, x) for pk, x in zip(pick, sel)]
            gate = [jnp.where(pk, -jnp.inf, x) for pk, x in zip(pick, gate)]
        pad = jnp.zeros((LANES - nblk, blk), BF16)
        pen_t = [jnp.concatenate([jnp.where((blk_id == i) | (fp & (x > 0.0)), 0.0, NEG_INF).astype(BF16), pad], axis=0)
                 for i, fp, x in zip(ids, fully_past, sel)]
        pen = [_dot_nt(eye, x).astype(BF16) for x in pen_t]
        for rows, x, y in zip(windows, q, pen):
            qa_ref[rows, :LANES] = x
            qa_ref[rows, LANES:] = y
        return 0

    lax.fori_loop(0, nblk // GATE_UNROLL, gating, 0)


    def aligned(x, m):
        return x if isinstance(x, int) else pl.multiple_of(x, m)

    def pair_bias(d):
        return jnp.concatenate([jnp.concatenate([tb_ref[d], tb_ref[d - 1]], axis=1),
                                jnp.concatenate([tb_ref[d + 1], tb_ref[d]], axis=1)], axis=0)

    def soften(t, j, slot):
        s_base = aligned(j * wide, wide)
        q_base = s_base if t is None else aligned((t + 1 + j) * wide, wide)
        for st in range(wide // MOBA_STRIP):
            off = st * MOBA_STRIP
            s = sc_ref[pl.ds(s_base + off, MOBA_STRIP), :]
            state = pl.ds(q_base + off, MOBA_STRIP)
            m = jnp.broadcast_to(jnp.max(s, axis=-1, keepdims=True), (MOBA_STRIP, LANES))
            if t is not None:
                m_old = m_ref[state, :]
                m = jnp.maximum(m_old, m)
                al_ref[slot, off:off + MOBA_STRIP, :] = jnp.exp2(m_old - m)
            p_ref[slot, off:off + MOBA_STRIP, :] = jnp.exp2(s - jnp.concatenate([m] * (wide // LANES), axis=1)).astype(BF16)
            m_ref[state, :] = m

    def accumulate(t, j, slot):
        if t is None:
            rows = _rows(j, wide)
            accl_ref[rows, :] = _dot(p_ref[slot], va_ref[rows, :])
        else:
            rows = _rows(t + 1 + j, wide)
            al = al_ref[slot]
            accl_ref[rows, :] = (jnp.concatenate([al, al], axis=1) * accl_ref[rows, :]
                                 + _dot(p_ref[slot], va_ref[t * wide:(t + 1) * wide, :]))

    def phase(t, n):
        soften(t, 0, 0)

        def body(j, _):
            accumulate(t, j - 1, (j - 1) % 2)
            soften(t, j, j % 2)
            return 0

        lax.fori_loop(1, n, body, 0)
        accumulate(t, n - 1, (n - 1) % 2)

    for u in range(npair):
        rows = slice(u * wide, (u + 1) * wide)
        sc_ref[rows, :] = _dot_nt(qa_ref[rows, :], ka_ref[rows, :]) + ob_ref[...]
    phase(None, npair)
    for t in range(npair - 1):
        n = npair - 1 - t
        for j in range(n):
            rows = slice((t + 1 + j) * wide, (t + 2 + j) * wide)
            sc_ref[j * wide:(j + 1) * wide, :] = (_dot_nt(qa_ref[rows, :], ka_ref[t * wide:(t + 1) * wide, :])
                                                  + pair_bias(2 + 2 * j))
        phase(t, n)


    def finish(i, _):
        rows = _rows(i, blk)
        o_ref[0, rows, :] = (accl_ref[rows, :LANES] / accl_ref[rows, LANES:]).astype(BF16)
        return 0

    lax.fori_loop(0, nblk, finish, 0)


def _moba_attention(qkv, tab):
    B, _, S, _ = qkv.shape
    blk = MOBA_BLOCK
    nblk = S // blk
    assert nblk % 2 == 0 and nblk % GATE_UNROLL == 0 and nblk <= LANES
    n_sel = min(MOBA_TOPK, nblk - 1)
    head = lambda slot: pl.BlockSpec((1, 1, S, LANES), lambda h, b: (b, slot + h, 0, 0))
    return pl.pallas_call(
        functools.partial(_moba_kernel, nblk=nblk, n_sel=n_sel),
        grid=(H_B, B),
        in_specs=[head(MB_Q), head(MB_K), head(MB_V), pl.BlockSpec((1, nblk, 2 * blk), lambda h, b: (h, 0, 0))],
        out_specs=pl.BlockSpec((1, S, LANES), lambda h, b: (b, 0, h)),
        out_shape=jax.ShapeDtypeStruct((B, S, H_B * HEAD_DIM), BF16),
        scratch_shapes=[pltpu.VMEM((nblk, blk, blk), F32), pltpu.VMEM((2 * blk, 2 * blk), F32),
                        pltpu.VMEM((S, 2 * LANES), BF16), pltpu.VMEM((S, 2 * LANES), BF16),
                        pltpu.VMEM((S, 2 * LANES), BF16), pltpu.VMEM((S, LANES), F32),
                        pltpu.VMEM((S, 2 * LANES), F32), pltpu.VMEM((S, 2 * blk), F32),
                        pltpu.VMEM((2, 2 * blk, 2 * blk), BF16), pltpu.VMEM((2, 2 * blk, LANES), F32)],
        compiler_params=_params(("arbitrary", "arbitrary")),
        name="moba_attention",
    )(qkv, qkv, qkv, tab)


DIL_UNROLL = 8


def _dilated_kernel(q1_ref, k1_ref, v1_ref, q4_ref, k4_ref, v4_ref, q16_ref, k16_ref, v16_ref, tab_ref, o_ref,
                    c4o_ref, c4l_ref, c16o_ref, c16l_ref, s4o_ref, s4l_ref, s16o_ref, s16l_ref, *, seq):
    bb = BAND_BLOCK
    qi = lax.broadcasted_iota(jnp.int32, (bb, 2 * bb), 0)
    kj = lax.broadcasted_iota(jnp.int32, (bb, 2 * bb), 1)
    visible = (kj >= qi) & (kj <= qi + bb)

    def pattern(g, r, q_ref, k_ref, v_ref, emit):
        bias = jnp.where(visible, _toeplitz(tab_ref[0, g:g + 1, :], bb, 0), NEG_INF)
        bias_first = jnp.concatenate([bias[:, bb:], jnp.full((bb, bb), NEG_INF, F32)], axis=1)
        nb = seq // r // bb

        def body(t, _):
            blocks = [((t * DIL_UNROLL + k) // nb, (t * DIL_UNROLL + k) % nb) for k in range(DIL_UNROLL)]
            keys = [_rows(jnp.maximum(n - 1, 0), bb, 2 * bb) for _, n in blocks]
            s = [_dot_nt(q_ref[0, 0, c, _rows(n, bb), :], k_ref[0, 0, c, kw, :]) + jnp.where(n == 0, bias_first, bias)
                 for (c, n), kw in zip(blocks, keys)]
            m = [jnp.max(x, axis=-1, keepdims=True) for x in s]
            p = [jnp.exp2(x - mx) for x, mx in zip(s, m)]
            den = [jnp.sum(x, axis=-1, keepdims=True) for x in p]
            o = [_dot(x.astype(BF16), v_ref[0, 0, c, kw, :]) for x, (c, _), kw in zip(p, blocks, keys)]
            for (c, n), ox, dx, mx in zip(blocks, o, den, m):
                emit(c, n, ox / dx, mx + jnp.log2(dx))
            return 0

        lax.fori_loop(0, r * nb // DIL_UNROLL, body, 0)

    def class_major(o_cm, l_cm):
        def emit(c, n, o, lse):
            o_cm[c, _rows(n, bb), :] = o
            l_cm[c, _rows(n, bb), :] = jnp.broadcast_to(lse, (bb, LANES))
        return emit

    def to_sequence(cm4, sq):
        for c in range(4):
            sq[pl.ds(c, seq // 4, stride=4), :] = cm4[c]

    def to_classes_of_4(cm16, cm4):
        for c in range(4):
            for k in range(4):
                cm4[c, pl.ds(k, seq // 16, stride=4), :] = cm16[c + 4 * k]

    pattern(2, 16, q16_ref, k16_ref, v16_ref, class_major(c16o_ref, c16l_ref))
    for cm16, sq in ((c16o_ref, s16o_ref), (c16l_ref, s16l_ref)):
        to_classes_of_4(cm16, c4o_ref)
        to_sequence(c4o_ref, sq)
    pattern(1, 4, q4_ref, k4_ref, v4_ref, class_major(c4o_ref, c4l_ref))
    to_sequence(c4o_ref, s4o_ref)
    to_sequence(c4l_ref, s4l_ref)

    def merge(c, n, o1, l1):
        rows = _rows(n, bb)
        l4, l16 = s4l_ref[rows, :], s16l_ref[rows, :]
        top = jnp.maximum(jnp.maximum(l1, l4), l16)
        e1, e4, e16 = jnp.exp2(l1 - top), jnp.exp2(l4 - top), jnp.exp2(l16 - top)
        mix = e1 * o1 + e4 * s4o_ref[rows, :] + e16 * s16o_ref[rows, :]
        o_ref[0, rows, :] = (mix / (e1 + e4 + e16)).astype(BF16)

    pattern(0, 1, q1_ref, k1_ref, v1_ref, merge)


def _dilated_attention(d1, d4, d16, tabs):
    B, S = d1.shape[0], d1.shape[3]
    assert [r for _, r in DILATIONS] == [1, 4, 16] and all(w // r == BAND_BLOCK for w, r in DILATIONS)
    assert (S // BAND_BLOCK) % DIL_UNROLL == 0
    head = lambda r, part: pl.BlockSpec((1, 1, r, S // r, LANES), lambda b, h: (b, part * H_C + h, 0, 0, 0))
    class_major = lambda r: pltpu.VMEM((r, S // r, LANES), F32)
    return pl.pallas_call(
        functools.partial(_dilated_kernel, seq=S),
        grid=(B, H_C),
        in_specs=[head(r, part) for r in (1, 4, 16) for part in range(3)]
        + [pl.BlockSpec((1, len(DILATIONS), 2 * BAND_BLOCK), lambda b, h: (h, 0, 0))],
        out_specs=pl.BlockSpec((1, S, LANES), lambda b, h: (b, 0, h)),
        out_shape=jax.ShapeDtypeStruct((B, S, H_C * HEAD_DIM), BF16),
        scratch_shapes=[class_major(4), class_major(4), class_major(16), class_major(16)]
        + [pltpu.VMEM((S, LANES), F32)] * 4,
        compiler_params=_params(("arbitrary", "arbitrary")),
        name="dilated_attention",
    )(d1, d1, d1, d4, d4, d4, d16, d16, d16, tabs)


def _merge_kernel(oa_ref, ob_ref, oc_ref, g0_ref, g1_ref, g2_ref, wbr_ref, wo_ref, x_ref, out_ref):
    ea, eb = H_A * HEAD_DIM, (H_A + H_B) * HEAD_DIM
    merged = (g0_ref[0].astype(F32) * _dot(oa_ref[0], wbr_ref[:ea, :])
              + g1_ref[0].astype(F32) * _dot(ob_ref[0], wbr_ref[ea:eb, :])
              + g2_ref[0].astype(F32) * _dot(oc_ref[0], wbr_ref[eb:, :]))
    out_ref[0] = x_ref[0] + _dot(merged.astype(BF16), wo_ref[...])


def _merge_out(oa, ob, oc, gates, w_branch, wo, layer, x, *, tm):
    B, S, D = x.shape
    rows = lambda w: pl.BlockSpec((1, tm, w), lambda b, s: (b, s, 0))
    gate = lambda g: pl.BlockSpec((1, tm, D), lambda b, s: (b, s, g))
    resident = lambda w: pl.BlockSpec((None,) + w.shape[1:], lambda b, s: (layer, 0, 0), pipeline_mode=pl.Buffered(1))
    return pl.pallas_call(
        _merge_kernel,
        grid=(B, S // tm),
        in_specs=[rows(oa.shape[-1]), rows(ob.shape[-1]), rows(oc.shape[-1]), gate(0), gate(1), gate(2),
                  resident(w_branch), resident(wo), rows(D)],
        out_specs=rows(D),
        out_shape=jax.ShapeDtypeStruct((B, S, D), F32),
        compiler_params=_params(("arbitrary", "arbitrary")),
        name="merge_out",
    )(oa, ob, oc, gates, gates, gates, w_branch, wo, x)


FFN_SUB = 256


def _ffn_kernel(x_ref, g_ref, wg_ref, wu_ref, wd_ref, o_ref, h_ref):
    @pl.when(pl.program_id(2) == 0)
    def _():
        x = x_ref[0]
        h_ref[...] = (_rms_rows(x) * g_ref[...]).astype(BF16)
        o_ref[0] = x

    for mi in range(h_ref.shape[0] // FFN_SUB):
        rows = slice(mi * FFN_SUB, (mi + 1) * FFN_SUB)
        h = h_ref[rows, :]
        gate = _dot(h, wg_ref[...])
        up = _dot(h, wu_ref[...])
        act = (gate * jax.nn.sigmoid(gate) * up).astype(BF16)
        o_ref[0, rows, :] += _dot(act, wd_ref[...])


def _ffn(x, g, w_gu, w_down, layer, *, tm, tf):
    B, S, D = x.shape
    nf = D_FF // tf
    return pl.pallas_call(
        _ffn_kernel,
        grid=(B, S // tm, nf),
        in_specs=[
            pl.BlockSpec((1, tm, D), lambda b, s, f: (b, s, 0)),
            pl.BlockSpec((1, D), lambda b, s, f: (0, 0)),
            pl.BlockSpec((None, D, tf), lambda b, s, f: (layer, 0, f)),
            pl.BlockSpec((None, D, tf), lambda b, s, f: (layer, 0, nf + f)),
            pl.BlockSpec((None, tf, D), lambda b, s, f: (layer, f, 0)),
        ],
        out_specs=pl.BlockSpec((1, tm, D), lambda b, s, f: (b, s, 0)),
        out_shape=jax.ShapeDtypeStruct((B, S, D), F32),
        scratch_shapes=[pltpu.VMEM((tm, D), BF16)],
        compiler_params=_params(("arbitrary", "arbitrary", "arbitrary")),
        name="ffn",
    )(x, g, w_gu, w_gu, w_down)


def _t5_bucket(dist):
    max_exact = N_BUCKETS // 2
    d = jnp.maximum(dist, 0)
    df = jnp.maximum(d, 1).astype(F32)
    large = max_exact + (jnp.log(df / max_exact) / math.log(MAX_DISTANCE / max_exact)
                         * (N_BUCKETS - max_exact)).astype(jnp.int32)
    large = jnp.minimum(large, N_BUCKETS - 1)
    return jnp.where(d < max_exact, d, large)


def _bias_tables(rel_bias, S):
    by_dist = rel_bias[_t5_bucket(jnp.arange(S))].T.astype(F32) * LOG2E
    nblk, blk, bb = S // MOBA_BLOCK, MOBA_BLOCK, BAND_BLOCK
    rev = jnp.pad(by_dist[:H_B], ((0, 0), (blk, 0)))[:, ::-1]
    tab_b = jnp.stack([rev[:, S - blk * (d + 1): S - blk * (d - 1)] for d in range(nblk)], axis=1)
    tabs_c = []
    for _, r in DILATIONS:
        near = by_dist[H_B:, ::r][:, :bb + 1]
        tabs_c.append(jnp.pad(near[:, ::-1], ((0, 0), (0, bb - 1))))
    return tab_b, jnp.stack(tabs_c, axis=1)


def _proj_colgain(q_gain, k_gain):
    scale = HEAD_DIM ** -0.5
    ones = lambda n: jnp.ones((n * HEAD_DIM,), F32)
    qg, kg = q_gain.astype(F32) * (scale * LOG2E), k_gain.astype(F32)
    return jnp.concatenate([
        ones(H_A) * (scale * LOG2E), ones(H_A), ones(H_A),
        qg[:H_B].reshape(-1), kg[:H_B].reshape(-1), ones(H_B),
        qg[H_B:].reshape(-1), kg[H_B:].reshape(-1), ones(H_C)])[None, :]


def _layer(x, layer, g_mix, w_in, q_gain, k_gain, w_branch, w_out, g_ffn, w_gu, w_down, tab_b, tabs_c):
    qkv, d1, d4, d16, gates = _proj(x, g_mix[None, :], w_in, layer, _proj_colgain(q_gain, k_gain), tm=1024)

    oa = _sb_attention(qkv, blk=256)
    ob = _moba_attention(qkv, tab_b)
    oc = _dilated_attention(d1, d4, d16, tabs_c)

    x = _merge_out(oa, ob, oc, gates, w_branch, w_out, layer, x, tm=256)
    return _ffn(x, g_ffn[None, :], w_gu, w_down, layer, tm=1024, tf=512)


def kernel(x, g_mix, w_in, q_gain, k_gain, w_branch, w_out, g_ffn, w_gu, w_down, rel_bias):
    depth = g_mix.shape[0]
    tab_b, tabs_c = _bias_tables(rel_bias, x.shape[1])
    w_in, w_branch, w_out, w_gu, w_down = (w.astype(BF16) for w in (w_in, w_branch, w_out, w_gu, w_down))
    for l in range(depth):
        x = _layer(x, l, g_mix[l], w_in, q_gain[l], k_gain[l], w_branch, w_out, g_ffn[l], w_gu, w_down, tab_b, tabs_c)
    return x
```

```python
import functools
import math

import jax
import jax.numpy as jnp
from jax import lax
from jax.experimental import pallas as pl
from jax.experimental.pallas import tpu as pltpu

D_MODEL = 2048
HEAD_DIM = 128
H_A = 4
H_B = 6
H_C = 6
N_HEADS = H_A + H_B + H_C
MIX_WIDTH = N_HEADS * HEAD_DIM
N_BRANCH = 3
D_FF = 5632
MOBA_BLOCK = 256
MOBA_TOPK = 3
DILATIONS = ((128, 1), (512, 4), (2048, 16))
BAND_BLOCK = 128
N_BUCKETS = 32
MAX_DISTANCE = 2048
RMS_EPS = 1e-6
NEG_INF = -1e30
LOG2E = math.log2(math.e)

LANES = 128
VMEM_LIMIT = 56 * 1024 * 1024

SB_SKIP_SUM = 110.0

F32 = jnp.float32
BF16 = jnp.bfloat16

SLOTS_MAIN = 3 * (H_A + H_B)
SLOTS_DIL = 3 * H_C
SB_Q, SB_K, SB_V = 0, H_A, 2 * H_A
MB_Q, MB_K, MB_V = 3 * H_A, 3 * H_A + H_B, 3 * H_A + 2 * H_B


def _params(sem):
    return pltpu.CompilerParams(dimension_semantics=sem, vmem_limit_bytes=VMEM_LIMIT)


def _dot(a, b):
    return jnp.dot(a, b, preferred_element_type=F32)


def _dot_nt(a, b):
    return lax.dot_general(a, b, (((1,), (1,)), ((), ())), preferred_element_type=F32)


def _rows(n, size, length=None):
    start = n * size if isinstance(n, int) else pl.multiple_of(n * size, size)
    return pl.ds(start, size if length is None else length)


def _rms_rows(x):
    return x * lax.rsqrt(jnp.mean(x * x, axis=-1, keepdims=True) + RMS_EPS)


PROJ_HEADS = 6
PROJ_TN = PROJ_HEADS * LANES
J_MAIN = SLOTS_MAIN // PROJ_HEADS
J_DIL = SLOTS_DIL // PROJ_HEADS
J_GATE = N_BRANCH * D_MODEL // PROJ_TN
NORMED_TILES = (2, 3, J_MAIN, J_MAIN + 1)
PROJ_SUB = 256
PROJ_PIECE = 2 * LANES
PROJ_PIECES_PER_TILE = PROJ_TN // PROJ_PIECE


def _proj_pieces():
    pairs = N_HEADS // 2
    block = lambda part, head: part * pairs + head // 2
    q, k, v = 0, 1, 2
    tiles = [
        [block(q, 0), block(q, 2), block(k, 0)], [block(k, 2), block(v, 0), block(v, 2)],
        *[[block(part, h) for h in range(H_A, H_A + H_B, 2)] for part in (q, k, v)],
        *[[block(part, h) for h in range(H_A + H_B, N_HEADS, 2)] for part in (q, k, v)],
    ]
    gates0 = 3 * pairs
    tiles += [[gates0 + PROJ_PIECES_PER_TILE * t + p for p in range(PROJ_PIECES_PER_TILE)] for t in range(J_GATE)]
    return [b for tile in tiles for b in tile]


def _proj_kernel(src_ref, x_ref, g_ref, w0_ref, w1_ref, w2_ref, cg_ref, main_ref, d1_ref, d4_ref, d16_ref, gates_ref,
                 h_ref, y_ref, y4_ref):
    del src_ref
    j = pl.program_id(2)
    tm = h_ref.shape[0]
    sub = PROJ_SUB

    is_gate = j >= J_MAIN + J_DIL
    is_dil = (j >= J_MAIN) & jnp.logical_not(is_gate)
    is_norm = functools.reduce(jnp.logical_or, [j == t for t in NORMED_TILES])

    def normed_rows(rows):
        h = (_rms_rows(x_ref[0, rows, :]) * g_ref[...]).astype(BF16)
        h_ref[rows, :] = h
        return h

    def sweep(epilogue, lhs=lambda rows: h_ref[rows, :]):
        for mi in range(tm // sub):
            rows = slice(mi * sub, (mi + 1) * sub)
            h = lhs(rows)
            epilogue(mi, rows, jnp.concatenate([_dot(h, w[...]) for w in (w0_ref, w1_ref, w2_ref)], axis=1))

    def heads(acc, normed):
        for hh in range(PROJ_HEADS):
            sl = slice(hh * LANES, (hh + 1) * LANES)
            yield hh, (_rms_rows(acc[:, sl]) if normed else acc[:, sl]) * cg_ref[:, sl]

    @pl.when(is_gate)
    def _():
        def epilogue(mi, rows, acc):
            gates_ref[0, rows, :] = jax.nn.sigmoid(acc).astype(BF16)
        sweep(epilogue)

    def to_main(normed):
        def epilogue(mi, rows, acc):
            for hh, y in heads(acc, normed):
                main_ref[0, hh, rows, :] = y.astype(BF16)
        return epilogue

    assert 0 not in NORMED_TILES

    @pl.when(j == 0)
    def _():
        sweep(to_main(False), normed_rows)

    for normed in (False, True):
        @pl.when((j > 0) & (j < J_MAIN) & (is_norm == normed))
        def _():
            sweep(to_main(normed))

        @pl.when(is_dil & (is_norm == normed))
        def _():
            def epilogue(mi, rows, acc):
                for hh, y in heads(acc, normed):
                    d1_ref[0, hh, 0, rows, :] = y.astype(BF16)
                    y_ref[mi % 2, hh] = y
                    for c in range(4):
                        y4 = y_ref[mi % 2, hh, pl.ds(c, sub // 4, stride=4), :]
                        d4_ref[0, hh, c, mi * (sub // 4):(mi + 1) * (sub // 4), :] = y4.astype(BF16)
                        y4_ref[mi % 2, hh, c] = y4
                        for k in range(4):
                            d16_ref[0, hh, c + 4 * k, mi * (sub // 16):(mi + 1) * (sub // 16), :] = (
                                y4_ref[mi % 2, hh, c, pl.ds(k, sub // 16, stride=4), :].astype(BF16))
            sweep(epilogue)


def _proj(x, g, w_in, layer, colgain, *, tm):
    B, S, D = x.shape
    tn, hp, npc = PROJ_TN, PROJ_HEADS, PROJ_PIECES_PER_TILE
    jd, jg = J_MAIN, J_MAIN + J_DIL
    piece = lambda p: pl.BlockSpec((None, D, PROJ_PIECE), lambda b, s, j, src: (layer, 0, src[npc * j + p]))
    dil = lambda r: pl.BlockSpec((1, hp, r, tm // r, LANES),
                                 lambda b, s, j, src: (b, jnp.clip(j - jd, 0, J_DIL - 1), 0, s, 0))
    dil_shape = lambda r: jax.ShapeDtypeStruct((B, SLOTS_DIL, r, S // r, LANES), BF16)
    grid_spec = pltpu.PrefetchScalarGridSpec(
        num_scalar_prefetch=1,
        grid=(B, S // tm, J_MAIN + J_DIL + J_GATE),
        in_specs=[
            pl.BlockSpec((1, tm, D), lambda b, s, j, src: (b, s, 0)),
            pl.BlockSpec((1, D), lambda b, s, j, src: (0, 0)),
            piece(0), piece(1), piece(2),
            pl.BlockSpec((1, tn), lambda b, s, j, src: (0, jnp.minimum(j, jg - 1))),
        ],
        out_specs=[
            pl.BlockSpec((1, hp, tm, LANES), lambda b, s, j, src: (b, jnp.minimum(j, jd - 1), s, 0)),
            dil(1), dil(4), dil(16),
            pl.BlockSpec((1, tm, tn), lambda b, s, j, src: (b, s, jnp.maximum(j - jg, 0))),
        ],
        scratch_shapes=[pltpu.VMEM((tm, D), BF16), pltpu.VMEM((2, hp, PROJ_SUB, LANES), F32),
                        pltpu.VMEM((2, hp, 4, PROJ_SUB // 4, LANES), F32)],
    )
    return pl.pallas_call(
        _proj_kernel,
        grid_spec=grid_spec,
        out_shape=[
            jax.ShapeDtypeStruct((B, SLOTS_MAIN, S, LANES), BF16),
            dil_shape(1), dil_shape(4), dil_shape(16),
            jax.ShapeDtypeStruct((B, S, N_BRANCH * D_MODEL), BF16),
        ],
        compiler_params=_params(("arbitrary", "arbitrary", "arbitrary")),
        name="proj",
    )(jnp.asarray(_proj_pieces(), jnp.int32), x, g, w_in, w_in, w_in, colgain)


def _softplus2(z):
    return jnp.maximum(z, 0.0) + jnp.log2(1.0 + jnp.exp2(-jnp.abs(z)))


def _sb_kernel(q_ref, k_ref, v_ref, u_ref, o_ref, *, blk):
    i = pl.program_id(1)
    u = u_ref[...]
    row = lax.broadcasted_iota(jnp.int32, (blk, blk), 0)
    col = lax.broadcasted_iota(jnp.int32, (blk, blk), 1)
    past = col < row

    def blocks(j, state, diag):
        keys = pl.ds(pl.multiple_of(j * blk, blk), blk)
        heads = range(H_A)
        z = [_dot_nt(q_ref[0, h], k_ref[0, h, keys, :]) for h in heads]
        sp = [_softplus2(z[h]) for h in heads]
        spm = [jnp.where(past, sp[h], 0.0) for h in heads] if diag else sp
        hi = [spm[h].astype(BF16) for h in heads]
        lo = [(spm[h] - hi[h].astype(F32)).astype(BF16) for h in heads]
        after = [_dot(hi[h], u) + _dot(lo[h], u) for h in heads]
        w = [jnp.exp2(z[h] - sp[h] - after[h] - state[h][0]) for h in heads]
        if diag:
            w = [jnp.where(past, w[h], 0.0) for h in heads]
        acc = [state[h][1] + _dot(w[h].astype(BF16), v_ref[0, h, keys, :]) for h in heads]
        c = [state[h][0] + after[h][:, :1] + spm[h][:, :1] for h in heads]
        return tuple(zip(c, acc))

    def smallest(state):
        return functools.reduce(jnp.minimum, [jnp.min(c) for c, _ in state])

    zeros = (jnp.zeros((blk, 1), F32), jnp.zeros((blk, HEAD_DIM), F32))
    state = blocks(i, (zeros,) * H_A, True)

    def cond(carry):
        j, c_min, _ = carry
        return (j >= 0) & (c_min < SB_SKIP_SUM * LOG2E)

    def body(carry):
        j, _, state = carry
        state = blocks(j, state, False)
        return j - 1, smallest(state), state

    _, _, state = lax.while_loop(cond, body, (i - 1, smallest(state), state))
    for h, (_, acc) in enumerate(state):
        o_ref[0, :, h * LANES:(h + 1) * LANES] = acc.astype(BF16)


def _sb_attention(qkv, *, blk):
    B, _, S, _ = qkv.shape
    tri = (jnp.arange(blk)[:, None] > jnp.arange(blk)[None, :]).astype(BF16)
    return pl.pallas_call(
        functools.partial(_sb_kernel, blk=blk),
        grid=(B, S // blk),
        in_specs=[
            pl.BlockSpec((1, H_A, blk, LANES), lambda b, i: (b, SB_Q // H_A, i, 0)),
            pl.BlockSpec((1, H_A, S, LANES), lambda b, i: (b, SB_K // H_A, 0, 0)),
            pl.BlockSpec((1, H_A, S, LANES), lambda b, i: (b, SB_V // H_A, 0, 0)),
            pl.BlockSpec((blk, blk), lambda b, i: (0, 0)),
        ],
        out_specs=pl.BlockSpec((1, blk, H_A * LANES), lambda b, i: (b, i, 0)),
        out_shape=jax.ShapeDtypeStruct((B, S, H_A * HEAD_DIM), BF16),
        compiler_params=_params(("arbitrary", "arbitrary")),
        name="sb_attention",
    )(qkv, qkv, qkv, tri)


def _toeplitz(row_vals, n_rows, shift):
    full = jnp.broadcast_to(row_vals, (n_rows, row_vals.shape[1]))
    return pltpu.roll(full, shift, 1, stride=1, stride_axis=0)


MOBA_STRIP = 32
GATE_UNROLL = 4


def _moba_kernel(q_ref, k_ref, v_ref, tab_ref, o_ref, tb_ref, ob_ref, ka_ref, va_ref, qa_ref, m_ref, accl_ref, sc_ref,
                 p_ref, al_ref, *, nblk, n_sel):
    blk = MOBA_BLOCK
    wide = 2 * blk
    npair = nblk // 2

    row = lax.broadcasted_iota(jnp.int32, (blk, blk), 0)
    col = lax.broadcasted_iota(jnp.int32, (blk, blk), 1)

    @pl.when(pl.program_id(1) == 0)
    def _():
        for d in range(nblk):
            tb_ref[d] = _toeplitz(tab_ref[0, d:d + 1, :], blk, blk + 1)[:, :blk]
        diag = jnp.where(col <= row, tb_ref[0], NEG_INF)
        ob_ref[:blk, :blk] = diag
        ob_ref[:blk, blk:] = jnp.zeros((blk, blk), F32)
        ob_ref[blk:, :blk] = tb_ref[1]
        ob_ref[blk:, blk:] = diag

    lane = lax.broadcasted_iota(jnp.int32, (blk, LANES), 1)
    va_ref[:, LANES:] = jnp.ones((va_ref.shape[0], LANES), BF16)
    means = []
    for n in range(nblk):
        rows = slice(n * blk, (n + 1) * blk)
        kb = k_ref[0, 0, rows, :]
        means.append(jnp.mean(kb.astype(F32), axis=0, keepdims=True))
        ka_ref[rows, :LANES] = kb
        ka_ref[rows, LANES:] = jnp.where(lane == n, 1.0, 0.0).astype(BF16)
        va_ref[rows, :LANES] = v_ref[0, 0, rows, :]
    km = jnp.concatenate(means, axis=0).astype(BF16)

    blk_id = lax.broadcasted_iota(jnp.int32, (nblk, blk), 0)
    eye = jnp.where(row == col, 1.0, 0.0).astype(BF16)

    def gating(g, _):
        ids = [g * GATE_UNROLL + k for k in range(GATE_UNROLL)]
        windows = [pl.ds(pl.multiple_of(g * (GATE_UNROLL * blk), GATE_UNROLL * blk) + k * blk, blk)
                   for k in range(GATE_UNROLL)]
        q = [q_ref[0, 0, rows, :] for rows in windows]
        fully_past = [blk_id < i for i in ids]
        gate = [jnp.where(fp, _dot_nt(km, x), NEG_INF) for fp, x in zip(fully_past, q)]
        sel = [jnp.zeros((nblk, blk), F32) for _ in ids]
        for _ in range(n_sel):
            top = [jnp.max(x, axis=0, keepdims=True) for x in gate]
            first = [jnp.min(jnp.where(x == t, blk_id, nblk), axis=0, keepdims=True)
                     for x, t in zip(gate, top)]
            pick = [blk_id == f for f in first]
            sel = [jnp.where(pk, 1.0, x) for pk, x in zip(pick, sel)]
            gate = [jnp.where(pk, -jnp.inf, x) for pk, x in zip(pick, gate)]
        pad = jnp.zeros((LANES - nblk, blk), BF16)
        pen_t = [jnp.concatenate([jnp.where((blk_id == i) | (fp & (x > 0.0)), 0.0, NEG_INF).astype(BF16), pad], axis=0)
                 for i, fp, x in zip(ids, fully_past, sel)]
        pen = [_dot_nt(eye, x).astype(BF16) for x in pen_t]
        for rows, x, y in zip(windows, q, pen):
            qa_ref[rows, :LANES] = x
            qa_ref[rows, LANES:] = y
        return 0

    lax.fori_loop(0, nblk // GATE_UNROLL, gating, 0)


    def aligned(x, m):
        return x if isinstance(x, int) else pl.multiple_of(x, m)

    def pair_bias(d):
        return jnp.concatenate([jnp.concatenate([tb_ref[d], tb_ref[d - 1]], axis=1),
                                jnp.concatenate([tb_ref[d + 1], tb_ref[d]], axis=1)], axis=0)

    def soften(t, j, slot):
        s_base = aligned(j * wide, wide)
        q_base = s_base if t is None else aligned((t + 1 + j) * wide, wide)
        for st in range(wide // MOBA_STRIP):
            off = st * MOBA_STRIP
            s = sc_ref[pl.ds(s_base + off, MOBA_STRIP), :]
            state = pl.ds(q_base + off, MOBA_STRIP)
            m = jnp.broadcast_to(jnp.max(s, axis=-1, keepdims=True), (MOBA_STRIP, LANES))
            if t is not None:
                m_old = m_ref[state, :]
                m = jnp.maximum(m_old, m)
                al_ref[slot, off:off + MOBA_STRIP, :] = jnp.exp2(m_old - m)
            p_ref[slot, off:off + MOBA_STRIP, :] = jnp.exp2(s - jnp.concatenate([m] * (wide // LANES), axis=1)).astype(BF16)
            m_ref[state, :] = m

    def accumulate(t, j, slot):
        if t is None:
            rows = _rows(j, wide)
            accl_ref[rows, :] = _dot(p_ref[slot], va_ref[rows, :])
        else:
            rows = _rows(t + 1 + j, wide)
            al = al_ref[slot]
            accl_ref[rows, :] = (jnp.concatenate([al, al], axis=1) * accl_ref[rows, :]
                                 + _dot(p_ref[slot], va_ref[t * wide:(t + 1) * wide, :]))

    def phase(t, n, score):
        for j in range(n):
            score(j)
        soften(t, 0, 0)

        def body(j, _):
            accumulate(t, j - 1, (j - 1) % 2)
            soften(t, j, j % 2)
            return 0

        lax.fori_loop(1, n, body, 0)
        accumulate(t, n - 1, (n - 1) % 2)

    def own_score(u):
        rows = slice(u * wide, (u + 1) * wide)
        sc_ref[rows, :] = _dot_nt(qa_ref[rows, :], ka_ref[rows, :]) + ob_ref[...]

    phase(None, npair, own_score)
    for t in range(npair - 1):
        def past_score(j):
            rows = slice((t + 1 + j) * wide, (t + 2 + j) * wide)
            sc_ref[j * wide:(j + 1) * wide, :] = (_dot_nt(qa_ref[rows, :], ka_ref[t * wide:(t + 1) * wide, :])
                                                  + pair_bias(2 + 2 * j))

        phase(t, npair - 1 - t, past_score)


    def finish(i, _):
        rows = _rows(i, blk)
        o_ref[0, rows, :] = (accl_ref[rows, :LANES] / accl_ref[rows, LANES:]).astype(BF16)
        return 0

    lax.fori_loop(0, nblk, finish, 0)


def _moba_attention(qkv, tab):
    B, _, S, _ = qkv.shape
    blk = MOBA_BLOCK
    nblk = S // blk
    assert nblk % 2 == 0 and nblk % GATE_UNROLL == 0 and nblk <= LANES
    n_sel = min(MOBA_TOPK, nblk - 1)
    head = lambda slot: pl.BlockSpec((1, 1, S, LANES), lambda h, b: (b, slot + h, 0, 0))
    return pl.pallas_call(
        functools.partial(_moba_kernel, nblk=nblk, n_sel=n_sel),
        grid=(H_B, B),
        in_specs=[head(MB_Q), head(MB_K), head(MB_V), pl.BlockSpec((1, nblk, 2 * blk), lambda h, b: (h, 0, 0))],
        out_specs=pl.BlockSpec((1, S, LANES), lambda h, b: (b, 0, h)),
        out_shape=jax.ShapeDtypeStruct((B, S, H_B * HEAD_DIM), BF16),
        scratch_shapes=[pltpu.VMEM((nblk, blk, blk), F32), pltpu.VMEM((2 * blk, 2 * blk), F32),
                        pltpu.VMEM((S, 2 * LANES), BF16), pltpu.VMEM((S, 2 * LANES), BF16),
                        pltpu.VMEM((S, 2 * LANES), BF16), pltpu.VMEM((S, LANES), F32),
                        pltpu.VMEM((S, 2 * LANES), F32), pltpu.VMEM((S, 2 * blk), F32),
                        pltpu.VMEM((2, 2 * blk, 2 * blk), BF16), pltpu.VMEM((2, 2 * blk, LANES), F32)],
        compiler_params=_params(("arbitrary", "arbitrary")),
        name="moba_attention",
    )(qkv, qkv, qkv, tab)


DIL_UNROLL = 16


def _dilated_kernel(q1_ref, k1_ref, v1_ref, q4_ref, k4_ref, v4_ref, q16_ref, k16_ref, v16_ref, tab_ref, o_ref,
                    c4o_ref, c4l_ref, c16o_ref, c16l_ref, s4o_ref, s4l_ref, s16o_ref, s16l_ref, *, seq):
    bb = BAND_BLOCK
    qi = lax.broadcasted_iota(jnp.int32, (bb, 2 * bb), 0)
    kj = lax.broadcasted_iota(jnp.int32, (bb, 2 * bb), 1)
    visible = (kj >= qi) & (kj <= qi + bb)

    def pattern(g, r, q_ref, k_ref, v_ref, emit):
        bias = jnp.where(visible, _toeplitz(tab_ref[0, g:g + 1, :], bb, 0), NEG_INF)
        bias_first = jnp.concatenate([bias[:, bb:], jnp.full((bb, bb), NEG_INF, F32)], axis=1)
        nb = seq // r // bb

        def body(t, _):
            blocks = [((t * DIL_UNROLL + k) // nb, (t * DIL_UNROLL + k) % nb) for k in range(DIL_UNROLL)]
            keys = [_rows(jnp.maximum(n - 1, 0), bb, 2 * bb) for _, n in blocks]
            s = [_dot_nt(q_ref[0, 0, c, _rows(n, bb), :], k_ref[0, 0, c, kw, :]) + jnp.where(n == 0, bias_first, bias)
                 for (c, n), kw in zip(blocks, keys)]
            m = [jnp.max(x, axis=-1, keepdims=True) for x in s]
            p = [jnp.exp2(x - mx) for x, mx in zip(s, m)]
            den = [jnp.sum(x, axis=-1, keepdims=True) for x in p]
            o = [_dot(x.astype(BF16), v_ref[0, 0, c, kw, :]) for x, (c, _), kw in zip(p, blocks, keys)]
            for (c, n), ox, dx, mx in zip(blocks, o, den, m):
                emit(c, n, ox / dx, mx + jnp.log2(dx))
            return 0

        lax.fori_loop(0, r * nb // DIL_UNROLL, body, 0)

    def class_major(o_cm, l_cm):
        def emit(c, n, o, lse):
            o_cm[c, _rows(n, bb), :] = o
            l_cm[c, _rows(n, bb), :] = jnp.broadcast_to(lse, (bb, LANES))
        return emit

    def to_sequence(cm4, sq):
        for c in range(4):
            sq[pl.ds(c, seq // 4, stride=4), :] = cm4[c]

    def to_classes_of_4(cm16, cm4):
        for c in range(4):
            for k in range(4):
                cm4[c, pl.ds(k, seq // 16, stride=4), :] = cm16[c + 4 * k]

    pattern(2, 16, q16_ref, k16_ref, v16_ref, class_major(c16o_ref, c16l_ref))
    for cm16, sq in ((c16o_ref, s16o_ref), (c16l_ref, s16l_ref)):
        to_classes_of_4(cm16, c4o_ref)
        to_sequence(c4o_ref, sq)
    pattern(1, 4, q4_ref, k4_ref, v4_ref, class_major(c4o_ref, c4l_ref))
    to_sequence(c4o_ref, s4o_ref)
    to_sequence(c4l_ref, s4l_ref)

    def merge(c, n, o1, l1):
        rows = _rows(n, bb)
        l4, l16 = s4l_ref[rows, :], s16l_ref[rows, :]
        top = jnp.maximum(jnp.maximum(l1, l4), l16)
        e1, e4, e16 = jnp.exp2(l1 - top), jnp.exp2(l4 - top), jnp.exp2(l16 - top)
        mix = e1 * o1 + e4 * s4o_ref[rows, :] + e16 * s16o_ref[rows, :]
        o_ref[0, rows, :] = (mix / (e1 + e4 + e16)).astype(BF16)

    pattern(0, 1, q1_ref, k1_ref, v1_ref, merge)


def _dilated_attention(d1, d4, d16, tabs):
    B, S = d1.shape[0], d1.shape[3]
    assert [r for _, r in DILATIONS] == [1, 4, 16] and all(w // r == BAND_BLOCK for w, r in DILATIONS)
    assert (S // BAND_BLOCK) % DIL_UNROLL == 0
    head = lambda r, part: pl.BlockSpec((1, 1, r, S // r, LANES), lambda b, h: (b, part * H_C + h, 0, 0, 0))
    class_major = lambda r: pltpu.VMEM((r, S // r, LANES), F32)
    return pl.pallas_call(
        functools.partial(_dilated_kernel, seq=S),
        grid=(B, H_C),
        in_specs=[head(r, part) for r in (1, 4, 16) for part in range(3)]
        + [pl.BlockSpec((1, len(DILATIONS), 2 * BAND_BLOCK), lambda b, h: (h, 0, 0))],
        out_specs=pl.BlockSpec((1, S, LANES), lambda b, h: (b, 0, h)),
        out_shape=jax.ShapeDtypeStruct((B, S, H_C * HEAD_DIM), BF16),
        scratch_shapes=[class_major(4), class_major(4), class_major(16), class_major(16)]
        + [pltpu.VMEM((S, LANES), F32)] * 4,
        compiler_params=_params(("arbitrary", "arbitrary")),
        name="dilated_attention",
    )(d1, d1, d1, d4, d4, d4, d16, d16, d16, tabs)


def _merge_kernel(oa_ref, ob_ref, oc_ref, g0_ref, g1_ref, g2_ref, wbr_ref, wo_ref, x_ref, out_ref):
    ea, eb = H_A * HEAD_DIM, (H_A + H_B) * HEAD_DIM
    merged = (g0_ref[0].astype(F32) * _dot(oa_ref[0], wbr_ref[:ea, :])
              + g1_ref[0].astype(F32) * _dot(ob_ref[0], wbr_ref[ea:eb, :])
              + g2_ref[0].astype(F32) * _dot(oc_ref[0], wbr_ref[eb:, :]))
    out_ref[0] = x_ref[0] + _dot(merged.astype(BF16), wo_ref[...])


def _merge_out(oa, ob, oc, gates, w_branch, wo, layer, x, *, tm):
    B, S, D = x.shape
    rows = lambda w: pl.BlockSpec((1, tm, w), lambda b, s: (b, s, 0))
    gate = lambda g: pl.BlockSpec((1, tm, D), lambda b, s: (b, s, g))
    resident = lambda w: pl.BlockSpec((None,) + w.shape[1:], lambda b, s: (layer, 0, 0), pipeline_mode=pl.Buffered(1))
    return pl.pallas_call(
        _merge_kernel,
        grid=(B, S // tm),
        in_specs=[rows(oa.shape[-1]), rows(ob.shape[-1]), rows(oc.shape[-1]), gate(0), gate(1), gate(2),
                  resident(w_branch), resident(wo), rows(D)],
        out_specs=rows(D),
        out_shape=jax.ShapeDtypeStruct((B, S, D), F32),
        compiler_params=_params(("arbitrary", "arbitrary")),
        name="merge_out",
    )(oa, ob, oc, gates, gates, gates, w_branch, wo, x)


FFN_SUB = 256


def _ffn_kernel(x_ref, g_ref, wg_ref, wu_ref, wd_ref, o_ref, h_ref):
    first = pl.program_id(2) == 0

    def sweep(start):
        for mi in range(h_ref.shape[0] // FFN_SUB):
            rows = slice(mi * FFN_SUB, (mi + 1) * FFN_SUB)
            h, base = start(rows)
            gate = _dot(h, wg_ref[...])
            up = _dot(h, wu_ref[...])
            act = (gate * jax.nn.sigmoid(gate) * up).astype(BF16)
            o_ref[0, rows, :] = base + _dot(act, wd_ref[...])

    @pl.when(first)
    def _():
        def start(rows):
            x = x_ref[0, rows, :]
            h = (_rms_rows(x) * g_ref[...]).astype(BF16)
            h_ref[rows, :] = h
            return h, x
        sweep(start)

    @pl.when(jnp.logical_not(first))
    def _():
        sweep(lambda rows: (h_ref[rows, :], o_ref[0, rows, :]))


def _ffn(x, g, w_gu, w_down, layer, *, tm, tf):
    B, S, D = x.shape
    nf = D_FF // tf
    return pl.pallas_call(
        _ffn_kernel,
        grid=(B, S // tm, nf),
        in_specs=[
            pl.BlockSpec((1, tm, D), lambda b, s, f: (b, s, 0)),
            pl.BlockSpec((1, D), lambda b, s, f: (0, 0)),
            pl.BlockSpec((None, D, tf), lambda b, s, f: (layer, 0, f)),
            pl.BlockSpec((None, D, tf), lambda b, s, f: (layer, 0, nf + f)),
            pl.BlockSpec((None, tf, D), lambda b, s, f: (layer, f, 0)),
        ],
        out_specs=pl.BlockSpec((1, tm, D), lambda b, s, f: (b, s, 0)),
        out_shape=jax.ShapeDtypeStruct((B, S, D), F32),
        scratch_shapes=[pltpu.VMEM((tm, D), BF16)],
        compiler_params=_params(("arbitrary", "arbitrary", "arbitrary")),
        name="ffn",
    )(x, g, w_gu, w_gu, w_down)


def _t5_bucket(dist):
    max_exact = N_BUCKETS // 2
    d = jnp.maximum(dist, 0)
    df = jnp.maximum(d, 1).astype(F32)
    large = max_exact + (jnp.log(df / max_exact) / math.log(MAX_DISTANCE / max_exact)
                         * (N_BUCKETS - max_exact)).astype(jnp.int32)
    large = jnp.minimum(large, N_BUCKETS - 1)
    return jnp.where(d < max_exact, d, large)


def _bias_tables(rel_bias, S):
    by_dist = rel_bias[_t5_bucket(jnp.arange(S))].T.astype(F32) * LOG2E
    nblk, blk, bb = S // MOBA_BLOCK, MOBA_BLOCK, BAND_BLOCK
    rev = jnp.pad(by_dist[:H_B], ((0, 0), (blk, 0)))[:, ::-1]
    tab_b = jnp.stack([rev[:, S - blk * (d + 1): S - blk * (d - 1)] for d in range(nblk)], axis=1)
    tabs_c = []
    for _, r in DILATIONS:
        near = by_dist[H_B:, ::r][:, :bb + 1]
        tabs_c.append(jnp.pad(near[:, ::-1], ((0, 0), (0, bb - 1))))
    return tab_b, jnp.stack(tabs_c, axis=1)


def _proj_colgain(q_gain, k_gain):
    scale = HEAD_DIM ** -0.5
    ones = lambda n: jnp.ones((n * HEAD_DIM,), F32)
    qg, kg = q_gain.astype(F32) * (scale * LOG2E), k_gain.astype(F32)
    return jnp.concatenate([
        ones(H_A) * (scale * LOG2E), ones(H_A), ones(H_A),
        qg[:H_B].reshape(-1), kg[:H_B].reshape(-1), ones(H_B),
        qg[H_B:].reshape(-1), kg[H_B:].reshape(-1), ones(H_C)])[None, :]


def _layer(x, layer, g_mix, w_in, q_gain, k_gain, w_branch, w_out, g_ffn, w_gu, w_down, tab_b, tabs_c):
    qkv, d1, d4, d16, gates = _proj(x, g_mix[None, :], w_in, layer, _proj_colgain(q_gain, k_gain), tm=1024)

    oa = _sb_attention(qkv, blk=256)
    ob = _moba_attention(qkv, tab_b)
    oc = _dilated_attention(d1, d4, d16, tabs_c)

    x = _merge_out(oa, ob, oc, gates, w_branch, w_out, layer, x, tm=256)
    return _ffn(x, g_ffn[None, :], w_gu, w_down, layer, tm=1024, tf=512)


def kernel(x, g_mix, w_in, q_gain, k_gain, w_branch, w_out, g_ffn, w_gu, w_down, rel_bias):
    depth = g_mix.shape[0]
    tab_b, tabs_c = _bias_tables(rel_bias, x.shape[1])
    w_in, w_branch, w_out, w_gu, w_down = (w.astype(BF16) for w in (w_in, w_branch, w_out, w_gu, w_down))
    for l in range(depth):
        x = _layer(x, l, g_mix[l], w_in, q_gain[l], k_gain[l], w_branch, w_out, g_ffn[l], w_gu, w_down, tab_b, tabs_c)
    return x
```

```python
import functools
import math

import jax
import jax.numpy as jnp
from jax import lax
from jax.experimental import pallas as pl
from jax.experimental.pallas import tpu as pltpu

D_MODEL = 2048
HEAD_DIM = 128
H_A = 4
H_B = 6
H_C = 6
N_HEADS = H_A + H_B + H_C
MIX_WIDTH = N_HEADS * HEAD_DIM
N_BRANCH = 3
D_FF = 5632
MOBA_BLOCK = 256
MOBA_TOPK = 3
DILATIONS = ((128, 1), (512, 4), (2048, 16))
BAND_BLOCK = 128
N_BUCKETS = 32
MAX_DISTANCE = 2048
RMS_EPS = 1e-6
NEG_INF = -1e30
LOG2E = math.log2(math.e)

LANES = 128
VMEM_LIMIT = 56 * 1024 * 1024

SB_SKIP_SUM = 110.0

F32 = jnp.float32
BF16 = jnp.bfloat16

SLOTS_MAIN = 3 * (H_A + H_B)
SLOTS_DIL = 3 * H_C
SB_Q, SB_K, SB_V = 0, H_A, 2 * H_A
MB_Q, MB_K, MB_V = 3 * H_A, 3 * H_A + H_B, 3 * H_A + 2 * H_B


def _params(sem):
    return pltpu.CompilerParams(dimension_semantics=sem, vmem_limit_bytes=VMEM_LIMIT)


def _dot(a, b):
    return jnp.dot(a, b, preferred_element_type=F32)


def _dot_nt(a, b):
    return lax.dot_general(a, b, (((1,), (1,)), ((), ())), preferred_element_type=F32)


def _rows(n, size, length=None):
    start = n * size if isinstance(n, int) else pl.multiple_of(n * size, size)
    return pl.ds(start, size if length is None else length)


def _rms_rows(x):
    return x * lax.rsqrt(jnp.mean(x * x, axis=-1, keepdims=True) + RMS_EPS)


PROJ_HEADS = 6
PROJ_TN = PROJ_HEADS * LANES
J_MAIN = SLOTS_MAIN // PROJ_HEADS
J_DIL = SLOTS_DIL // PROJ_HEADS
J_GATE = N_BRANCH * D_MODEL // PROJ_TN
NORMED_TILES = (2, 3, J_MAIN, J_MAIN + 1)
PROJ_SUB = 256
PROJ_PIECE = 2 * LANES
PROJ_PIECES_PER_TILE = PROJ_TN // PROJ_PIECE


def _proj_pieces():
    pairs = N_HEADS // 2
    block = lambda part, head: part * pairs + head // 2
    q, k, v = 0, 1, 2
    tiles = [
        [block(q, 0), block(q, 2), block(k, 0)], [block(k, 2), block(v, 0), block(v, 2)],
        *[[block(part, h) for h in range(H_A, H_A + H_B, 2)] for part in (q, k, v)],
        *[[block(part, h) for h in range(H_A + H_B, N_HEADS, 2)] for part in (q, k, v)],
    ]
    gates0 = 3 * pairs
    tiles += [[gates0 + PROJ_PIECES_PER_TILE * t + p for p in range(PROJ_PIECES_PER_TILE)] for t in range(J_GATE)]
    return [b for tile in tiles for b in tile]


def _proj_kernel(src_ref, x_ref, g_ref, w0_ref, w1_ref, w2_ref, cg_ref, main_ref, d1_ref, d4_ref, d16_ref, gates_ref,
                 h_ref, y_ref, y4_ref):
    del src_ref
    j = pl.program_id(2)
    tm = h_ref.shape[0]
    sub = PROJ_SUB

    is_gate = j >= J_MAIN + J_DIL
    is_dil = (j >= J_MAIN) & jnp.logical_not(is_gate)
    is_norm = functools.reduce(jnp.logical_or, [j == t for t in NORMED_TILES])

    def normed_rows(rows):
        h = (_rms_rows(x_ref[0, rows, :]) * g_ref[...]).astype(BF16)
        h_ref[rows, :] = h
        return h

    def sweep(epilogue, lhs=lambda rows: h_ref[rows, :]):
        for mi in range(tm // sub):
            rows = slice(mi * sub, (mi + 1) * sub)
            h = lhs(rows)
            epilogue(mi, rows, jnp.concatenate([_dot(h, w[...]) for w in (w0_ref, w1_ref, w2_ref)], axis=1))

    def heads(acc, normed):
        for hh in range(PROJ_HEADS):
            sl = slice(hh * LANES, (hh + 1) * LANES)
            yield hh, (_rms_rows(acc[:, sl]) if normed else acc[:, sl]) * cg_ref[:, sl]

    @pl.when(is_gate)
    def _():
        def epilogue(mi, rows, acc):
            gates_ref[0, rows, :] = jax.nn.sigmoid(acc).astype(BF16)
        sweep(epilogue)

    def to_main(normed):
        def epilogue(mi, rows, acc):
            for hh, y in heads(acc, normed):
                main_ref[0, hh, rows, :] = y.astype(BF16)
        return epilogue

    assert 0 not in NORMED_TILES

    @pl.when(j == 0)
    def _():
        sweep(to_main(False), normed_rows)

    for normed in (False, True):
        @pl.when((j > 0) & (j < J_MAIN) & (is_norm == normed))
        def _():
            sweep(to_main(normed))

        @pl.when(is_dil & (is_norm == normed))
        def _():
            def epilogue(mi, rows, acc):
                for hh, y in heads(acc, normed):
                    d1_ref[0, hh, 0, rows, :] = y.astype(BF16)
                    y_ref[mi % 2, hh] = y
                    for c in range(4):
                        y4 = y_ref[mi % 2, hh, pl.ds(c, sub // 4, stride=4), :]
                        d4_ref[0, hh, c, mi * (sub // 4):(mi + 1) * (sub // 4), :] = y4.astype(BF16)
                        y4_ref[mi % 2, hh, c] = y4
                        for k in range(4):
                            d16_ref[0, hh, c + 4 * k, mi * (sub // 16):(mi + 1) * (sub // 16), :] = (
                                y4_ref[mi % 2, hh, c, pl.ds(k, sub // 16, stride=4), :].astype(BF16))
            sweep(epilogue)


def _proj(x, g, w_in, layer, colgain, *, tm):
    B, S, D = x.shape
    tn, hp, npc = PROJ_TN, PROJ_HEADS, PROJ_PIECES_PER_TILE
    jd, jg = J_MAIN, J_MAIN + J_DIL
    piece = lambda p: pl.BlockSpec((None, D, PROJ_PIECE), lambda b, s, j, src: (layer, 0, src[npc * j + p]))
    dil = lambda r: pl.BlockSpec((1, hp, r, tm // r, LANES),
                                 lambda b, s, j, src: (b, jnp.clip(j - jd, 0, J_DIL - 1), 0, s, 0))
    dil_shape = lambda r: jax.ShapeDtypeStruct((B, SLOTS_DIL, r, S // r, LANES), BF16)
    grid_spec = pltpu.PrefetchScalarGridSpec(
        num_scalar_prefetch=1,
        grid=(B, S // tm, J_MAIN + J_DIL + J_GATE),
        in_specs=[
            pl.BlockSpec((1, tm, D), lambda b, s, j, src: (b, s, 0)),
            pl.BlockSpec((1, D), lambda b, s, j, src: (0, 0)),
            piece(0), piece(1), piece(2),
            pl.BlockSpec((1, tn), lambda b, s, j, src: (0, jnp.minimum(j, jg - 1))),
        ],
        out_specs=[
            pl.BlockSpec((1, hp, tm, LANES), lambda b, s, j, src: (b, jnp.minimum(j, jd - 1), s, 0)),
            dil(1), dil(4), dil(16),
            pl.BlockSpec((1, tm, tn), lambda b, s, j, src: (b, s, jnp.maximum(j - jg, 0))),
        ],
        scratch_shapes=[pltpu.VMEM((tm, D), BF16), pltpu.VMEM((2, hp, PROJ_SUB, LANES), F32),
                        pltpu.VMEM((2, hp, 4, PROJ_SUB // 4, LANES), F32)],
    )
    return pl.pallas_call(
        _proj_kernel,
        grid_spec=grid_spec,
        out_shape=[
            jax.ShapeDtypeStruct((B, SLOTS_MAIN, S, LANES), BF16),
            dil_shape(1), dil_shape(4), dil_shape(16),
            jax.ShapeDtypeStruct((B, S, N_BRANCH * D_MODEL), BF16),
        ],
        compiler_params=_params(("arbitrary", "arbitrary", "arbitrary")),
        name="proj",
    )(jnp.asarray(_proj_pieces(), jnp.int32), x, g, w_in, w_in, w_in, colgain)


def _softplus2(z):
    return jnp.maximum(z, 0.0) + jnp.log2(1.0 + jnp.exp2(-jnp.abs(z)))


def _sb_kernel(q_ref, k_ref, v_ref, u_ref, o_ref, *, blk):
    i = pl.program_id(1)
    u = u_ref[...]
    row = lax.broadcasted_iota(jnp.int32, (blk, blk), 0)
    col = lax.broadcasted_iota(jnp.int32, (blk, blk), 1)
    past = col < row

    def blocks(j, state, diag):
        keys = pl.ds(pl.multiple_of(j * blk, blk), blk)
        heads = range(H_A)
        z = [_dot_nt(q_ref[0, h], k_ref[0, h, keys, :]) for h in heads]
        sp = [_softplus2(z[h]) for h in heads]
        spm = [jnp.where(past, sp[h], 0.0) for h in heads] if diag else sp
        hi = [spm[h].astype(BF16) for h in heads]
        lo = [(spm[h] - hi[h].astype(F32)).astype(BF16) for h in heads]
        after = [_dot(hi[h], u) + _dot(lo[h], u) for h in heads]
        w = [jnp.exp2(z[h] - sp[h] - after[h] - state[h][0]) for h in heads]
        if diag:
            w = [jnp.where(past, w[h], 0.0) for h in heads]
        acc = [state[h][1] + _dot(w[h].astype(BF16), v_ref[0, h, keys, :]) for h in heads]
        c = [state[h][0] + after[h][:, :1] + spm[h][:, :1] for h in heads]
        return tuple(zip(c, acc))

    def smallest(state):
        return functools.reduce(jnp.minimum, [jnp.min(c) for c, _ in state])

    zeros = (jnp.zeros((blk, 1), F32), jnp.zeros((blk, HEAD_DIM), F32))
    state = blocks(i, (zeros,) * H_A, True)

    def cond(carry):
        j, c_min, _ = carry
        return (j >= 0) & (c_min < SB_SKIP_SUM * LOG2E)

    def body(carry):
        j, _, state = carry
        state = blocks(j, state, False)
        return j - 1, smallest(state), state

    _, _, state = lax.while_loop(cond, body, (i - 1, smallest(state), state))
    for h, (_, acc) in enumerate(state):
        o_ref[0, :, h * LANES:(h + 1) * LANES] = acc.astype(BF16)


def _sb_attention(qkv, *, blk):
    B, _, S, _ = qkv.shape
    tri = (jnp.arange(blk)[:, None] > jnp.arange(blk)[None, :]).astype(BF16)
    return pl.pallas_call(
        functools.partial(_sb_kernel, blk=blk),
        grid=(B, S // blk),
        in_specs=[
            pl.BlockSpec((1, H_A, blk, LANES), lambda b, i: (b, SB_Q // H_A, i, 0)),
            pl.BlockSpec((1, H_A, S, LANES), lambda b, i: (b, SB_K // H_A, 0, 0)),
            pl.BlockSpec((1, H_A, S, LANES), lambda b, i: (b, SB_V // H_A, 0, 0)),
            pl.BlockSpec((blk, blk), lambda b, i: (0, 0)),
        ],
        out_specs=pl.BlockSpec((1, blk, H_A * LANES), lambda b, i: (b, i, 0)),
        out_shape=jax.ShapeDtypeStruct((B, S, H_A * HEAD_DIM), BF16),
        compiler_params=_params(("arbitrary", "arbitrary")),
        name="sb_attention",
    )(qkv, qkv, qkv, tri)


def _toeplitz(row_vals, n_rows, shift):
    full = jnp.broadcast_to(row_vals, (n_rows, row_vals.shape[1]))
    return pltpu.roll(full, shift, 1, stride=1, stride_axis=0)


MOBA_STRIP = 64
GATE_UNROLL = 8


def _moba_kernel(q_ref, k_ref, v_ref, tab_ref, o_ref, tb_ref, ob_ref, ka_ref, va_ref, qa_ref, m_ref, accl_ref, sc_ref,
                 p_ref, al_ref, *, nblk, n_sel):
    blk = MOBA_BLOCK
    wide = 2 * blk
    npair = nblk // 2

    row = lax.broadcasted_iota(jnp.int32, (blk, blk), 0)
    col = lax.broadcasted_iota(jnp.int32, (blk, blk), 1)

    @pl.when(pl.program_id(1) == 0)
    def _():
        for d in range(nblk):
            tb_ref[d] = _toeplitz(tab_ref[0, d:d + 1, :], blk, blk + 1)[:, :blk]
        diag = jnp.where(col <= row, tb_ref[0], NEG_INF)
        ob_ref[:blk, :blk] = diag
        ob_ref[:blk, blk:] = jnp.zeros((blk, blk), F32)
        ob_ref[blk:, :blk] = tb_ref[1]
        ob_ref[blk:, blk:] = diag

    lane = lax.broadcasted_iota(jnp.int32, (blk, LANES), 1)
    va_ref[:, LANES:] = jnp.ones((va_ref.shape[0], LANES), BF16)
    means = []
    for n in range(nblk):
        rows = slice(n * blk, (n + 1) * blk)
        kb = k_ref[0, 0, rows, :]
        means.append(jnp.mean(kb.astype(F32), axis=0, keepdims=True))
        ka_ref[rows, :LANES] = kb
        ka_ref[rows, LANES:] = jnp.where(lane == n, 1.0, 0.0).astype(BF16)
        va_ref[rows, :LANES] = v_ref[0, 0, rows, :]
    km = jnp.concatenate(means, axis=0).astype(BF16)

    blk_id = lax.broadcasted_iota(jnp.int32, (nblk, blk), 0)
    eye = jnp.where(row == col, 1.0, 0.0).astype(BF16)

    def gating(g, _):
        ids = [g * GATE_UNROLL + k for k in range(GATE_UNROLL)]
        windows = [pl.ds(pl.multiple_of(g * (GATE_UNROLL * blk), GATE_UNROLL * blk) + k * blk, blk)
                   for k in range(GATE_UNROLL)]
        q = [q_ref[0, 0, rows, :] for rows in windows]
        fully_past = [blk_id < i for i in ids]
        gate = [jnp.where(fp, _dot_nt(km, x), NEG_INF) for fp, x in zip(fully_past, q)]
        sel = [jnp.zeros((nblk, blk), F32) for _ in ids]
        for _ in range(n_sel):
            top = [jnp.max(x, axis=0, keepdims=True) for x in gate]
            first = [jnp.min(jnp.where(x == t, blk_id, nblk), axis=0, keepdims=True)
                     for x, t in zip(gate, top)]
            pick = [blk_id == f for f in first]
            sel = [jnp.where(pk, 1.0, x) for pk, x in zip(pick, sel)]
            gate = [jnp.where(pk, -jnp.inf, x) for pk, x in zip(pick, gate)]
        pad = jnp.zeros((LANES - nblk, blk), BF16)
        pen_t = [jnp.concatenate([jnp.where((blk_id == i) | (fp & (x > 0.0)), 0.0, NEG_INF).astype(BF16), pad], axis=0)
                 for i, fp, x in zip(ids, fully_past, sel)]
        pen = [_dot_nt(eye, x).astype(BF16) for x in pen_t]
        for rows, x, y in zip(windows, q, pen):
            qa_ref[rows, :LANES] = x
            qa_ref[rows, LANES:] = y
        return 0

    lax.fori_loop(0, nblk // GATE_UNROLL, gating, 0)


    def aligned(x, m):
        return x if isinstance(x, int) else pl.multiple_of(x, m)

    def pair_bias(d):
        return jnp.concatenate([jnp.concatenate([tb_ref[d], tb_ref[d - 1]], axis=1),
                                jnp.concatenate([tb_ref[d + 1], tb_ref[d]], axis=1)], axis=0)

    def soften(t, j, slot):
        s_base = aligned(j * wide, wide)
        q_base = s_base if t is None else aligned((t + 1 + j) * wide, wide)
        for st in range(wide // MOBA_STRIP):
            off = st * MOBA_STRIP
            s = sc_ref[pl.ds(s_base + off, MOBA_STRIP), :]
            state = pl.ds(q_base + off, MOBA_STRIP)
            m = jnp.broadcast_to(jnp.max(s, axis=-1, keepdims=True), (MOBA_STRIP, LANES))
            if t is not None:
                m_old = m_ref[state, :]
                m = jnp.maximum(m_old, m)
                al_ref[slot, off:off + MOBA_STRIP, :] = jnp.exp2(m_old - m)
            p_ref[slot, off:off + MOBA_STRIP, :] = jnp.exp2(s - jnp.concatenate([m] * (wide // LANES), axis=1)).astype(BF16)
            m_ref[state, :] = m

    def accumulate(t, j, slot):
        if t is None:
            rows = _rows(j, wide)
            accl_ref[rows, :] = _dot(p_ref[slot], va_ref[rows, :])
        else:
            rows = _rows(t + 1 + j, wide)
            al = al_ref[slot]
            accl_ref[rows, :] = (jnp.concatenate([al, al], axis=1) * accl_ref[rows, :]
                                 + _dot(p_ref[slot], va_ref[t * wide:(t + 1) * wide, :]))

    def phase(t, n, score):
        for j in range(n):
            score(j)
        soften(t, 0, 0)

        def body(j, _):
            accumulate(t, j - 1, (j - 1) % 2)
            soften(t, j, j % 2)
            return 0

        lax.fori_loop(1, n, body, 0)
        accumulate(t, n - 1, (n - 1) % 2)

    def own_score(u):
        rows = slice(u * wide, (u + 1) * wide)
        sc_ref[rows, :] = _dot_nt(qa_ref[rows, :], ka_ref[rows, :]) + ob_ref[...]

    phase(None, npair, own_score)
    for t in range(npair - 1):
        def past_score(j):
            rows = slice((t + 1 + j) * wide, (t + 2 + j) * wide)
            sc_ref[j * wide:(j + 1) * wide, :] = (_dot_nt(qa_ref[rows, :], ka_ref[t * wide:(t + 1) * wide, :])
                                                  + pair_bias(2 + 2 * j))

        phase(t, npair - 1 - t, past_score)


    def finish(i, _):
        rows = _rows(i, blk)
        o_ref[0, rows, :] = (accl_ref[rows, :LANES] / accl_ref[rows, LANES:]).astype(BF16)
        return 0

    lax.fori_loop(0, nblk, finish, 0)


def _moba_attention(qkv, tab):
    B, _, S, _ = qkv.shape
    blk = MOBA_BLOCK
    nblk = S // blk
    assert nblk % 2 == 0 and nblk % GATE_UNROLL == 0 and nblk <= LANES
    n_sel = min(MOBA_TOPK, nblk - 1)
    head = lambda slot: pl.BlockSpec((1, 1, S, LANES), lambda h, b: (b, slot + h, 0, 0))
    return pl.pallas_call(
        functools.partial(_moba_kernel, nblk=nblk, n_sel=n_sel),
        grid=(H_B, B),
        in_specs=[head(MB_Q), head(MB_K), head(MB_V), pl.BlockSpec((1, nblk, 2 * blk), lambda h, b: (h, 0, 0))],
        out_specs=pl.BlockSpec((1, S, LANES), lambda h, b: (b, 0, h)),
        out_shape=jax.ShapeDtypeStruct((B, S, H_B * HEAD_DIM), BF16),
        scratch_shapes=[pltpu.VMEM((nblk, blk, blk), F32), pltpu.VMEM((2 * blk, 2 * blk), F32),
                        pltpu.VMEM((S, 2 * LANES), BF16), pltpu.VMEM((S, 2 * LANES), BF16),
                        pltpu.VMEM((S, 2 * LANES), BF16), pltpu.VMEM((S, LANES), F32),
                        pltpu.VMEM((S, 2 * LANES), F32), pltpu.VMEM((S, 2 * blk), F32),
                        pltpu.VMEM((2, 2 * blk, 2 * blk), BF16), pltpu.VMEM((2, 2 * blk, LANES), F32)],
        compiler_params=_params(("arbitrary", "arbitrary")),
        name="moba_attention",
    )(qkv, qkv, qkv, tab)


DIL_UNROLL = 16


def _dilated_kernel(q1_ref, k1_ref, v1_ref, q4_ref, k4_ref, v4_ref, q16_ref, k16_ref, v16_ref, tab_ref, o_ref,
                    c4o_ref, c4l_ref, c16o_ref, c16l_ref, s4o_ref, s4l_ref, s16o_ref, s16l_ref, *, seq):
    bb = BAND_BLOCK
    qi = lax.broadcasted_iota(jnp.int32, (bb, 2 * bb), 0)
    kj = lax.broadcasted_iota(jnp.int32, (bb, 2 * bb), 1)
    visible = (kj >= qi) & (kj <= qi + bb)

    def pattern(g, r, q_ref, k_ref, v_ref, emit):
        bias = jnp.where(visible, _toeplitz(tab_ref[0, g:g + 1, :], bb, 0), NEG_INF)
        bias_first = jnp.concatenate([bias[:, bb:], jnp.full((bb, bb), NEG_INF, F32)], axis=1)
        nb = seq // r // bb

        def body(t, _):
            blocks = [((t * DIL_UNROLL + k) // nb, (t * DIL_UNROLL + k) % nb) for k in range(DIL_UNROLL)]
            keys = [_rows(jnp.maximum(n - 1, 0), bb, 2 * bb) for _, n in blocks]
            s = [_dot_nt(q_ref[0, 0, c, _rows(n, bb), :], k_ref[0, 0, c, kw, :]) + jnp.where(n == 0, bias_first, bias)
                 for (c, n), kw in zip(blocks, keys)]
            m = [jnp.max(x, axis=-1, keepdims=True) for x in s]
            p = [jnp.exp2(x - mx) for x, mx in zip(s, m)]
            den = [jnp.sum(x, axis=-1, keepdims=True) for x in p]
            o = [_dot(x.astype(BF16), v_ref[0, 0, c, kw, :]) for x, (c, _), kw in zip(p, blocks, keys)]
            for (c, n), ox, dx, mx in zip(blocks, o, den, m):
                emit(c, n, ox / dx, mx + jnp.log2(dx))
            return 0

        lax.fori_loop(0, r * nb // DIL_UNROLL, body, 0)

    def class_major(o_cm, l_cm):
        def emit(c, n, o, lse):
            o_cm[c, _rows(n, bb), :] = o
            l_cm[c, _rows(n, bb), :] = jnp.broadcast_to(lse, (bb, LANES))
        return emit

    def to_sequence(cm4, sq):
        for c in range(4):
            sq[pl.ds(c, seq // 4, stride=4), :] = cm4[c]

    def to_classes_of_4(cm16, cm4):
        for c in range(4):
            for k in range(4):
                cm4[c, pl.ds(k, seq // 16, stride=4), :] = cm16[c + 4 * k]

    pattern(2, 16, q16_ref, k16_ref, v16_ref, class_major(c16o_ref, c16l_ref))
    for cm16, sq in ((c16o_ref, s16o_ref), (c16l_ref, s16l_ref)):
        to_classes_of_4(cm16, c4o_ref)
        to_sequence(c4o_ref, sq)
    pattern(1, 4, q4_ref, k4_ref, v4_ref, class_major(c4o_ref, c4l_ref))
    to_sequence(c4o_ref, s4o_ref)
    to_sequence(c4l_ref, s4l_ref)

    def merge(c, n, o1, l1):
        rows = _rows(n, bb)
        l4, l16 = s4l_ref[rows, :], s16l_ref[rows, :]
        top = jnp.maximum(jnp.maximum(l1, l4), l16)
        e1, e4, e16 = jnp.exp2(l1 - top), jnp.exp2(l4 - top), jnp.exp2(l16 - top)
        mix = e1 * o1 + e4 * s4o_ref[rows, :] + e16 * s16o_ref[rows, :]
        o_ref[0, rows, :] = (mix / (e1 + e4 + e16)).astype(BF16)

    pattern(0, 1, q1_ref, k1_ref, v1_ref, merge)


def _dilated_attention(d1, d4, d16, tabs):
    B, S = d1.shape[0], d1.shape[3]
    assert [r for _, r in DILATIONS] == [1, 4, 16] and all(w // r == BAND_BLOCK for w, r in DILATIONS)
    assert (S // BAND_BLOCK) % DIL_UNROLL == 0
    head = lambda r, part: pl.BlockSpec((1, 1, r, S // r, LANES), lambda b, h: (b, part * H_C + h, 0, 0, 0))
    class_major = lambda r: pltpu.VMEM((r, S // r, LANES), F32)
    return pl.pallas_call(
        functools.partial(_dilated_kernel, seq=S),
        grid=(B, H_C),
        in_specs=[head(r, part) for r in (1, 4, 16) for part in range(3)]
        + [pl.BlockSpec((1, len(DILATIONS), 2 * BAND_BLOCK), lambda b, h: (h, 0, 0))],
        out_specs=pl.BlockSpec((1, S, LANES), lambda b, h: (b, 0, h)),
        out_shape=jax.ShapeDtypeStruct((B, S, H_C * HEAD_DIM), BF16),
        scratch_shapes=[class_major(4), class_major(4), class_major(16), class_major(16)]
        + [pltpu.VMEM((S, LANES), F32)] * 4,
        compiler_params=_params(("arbitrary", "arbitrary")),
        name="dilated_attention",
    )(d1, d1, d1, d4, d4, d4, d16, d16, d16, tabs)


def _merge_kernel(oa_ref, ob_ref, oc_ref, g0_ref, g1_ref, g2_ref, wbr_ref, wo_ref, x_ref, out_ref):
    ea, eb = H_A * HEAD_DIM, (H_A + H_B) * HEAD_DIM
    merged = (g0_ref[0].astype(F32) * _dot(oa_ref[0], wbr_ref[:ea, :])
              + g1_ref[0].astype(F32) * _dot(ob_ref[0], wbr_ref[ea:eb, :])
              + g2_ref[0].astype(F32) * _dot(oc_ref[0], wbr_ref[eb:, :]))
    out_ref[0] = x_ref[0] + _dot(merged.astype(BF16), wo_ref[...])


def _merge_out(oa, ob, oc, gates, w_branch, wo, layer, x, *, tm):
    B, S, D = x.shape
    rows = lambda w: pl.BlockSpec((1, tm, w), lambda b, s: (b, s, 0))
    gate = lambda g: pl.BlockSpec((1, tm, D), lambda b, s: (b, s, g))
    resident = lambda w: pl.BlockSpec((None,) + w.shape[1:], lambda b, s: (layer, 0, 0), pipeline_mode=pl.Buffered(1))
    return pl.pallas_call(
        _merge_kernel,
        grid=(B, S // tm),
        in_specs=[rows(oa.shape[-1]), rows(ob.shape[-1]), rows(oc.shape[-1]), gate(0), gate(1), gate(2),
                  resident(w_branch), resident(wo), rows(D)],
        out_specs=rows(D),
        out_shape=jax.ShapeDtypeStruct((B, S, D), F32),
        compiler_params=_params(("arbitrary", "arbitrary")),
        name="merge_out",
    )(oa, ob, oc, gates, gates, gates, w_branch, wo, x)


FFN_SUB = 256


def _ffn_kernel(x_ref, g_ref, wg_ref, wu_ref, wd_ref, o_ref, h_ref):
    first = pl.program_id(2) == 0

    def sweep(start):
        for mi in range(h_ref.shape[0] // FFN_SUB):
            rows = slice(mi * FFN_SUB, (mi + 1) * FFN_SUB)
            h, base = start(rows)
            gate = _dot(h, wg_ref[...])
            up = _dot(h, wu_ref[...])
            act = (gate * jax.nn.sigmoid(gate) * up).astype(BF16)
            o_ref[0, rows, :] = base + _dot(act, wd_ref[...])

    @pl.when(first)
    def _():
        def start(rows):
            x = x_ref[0, rows, :]
            h = (_rms_rows(x) * g_ref[...]).astype(BF16)
            h_ref[rows, :] = h
            return h, x
        sweep(start)

    @pl.when(jnp.logical_not(first))
    def _():
        sweep(lambda rows: (h_ref[rows, :], o_ref[0, rows, :]))


def _ffn(x, g, w_gu, w_down, layer, *, tm, tf):
    B, S, D = x.shape
    nf = D_FF // tf
    return pl.pallas_call(
        _ffn_kernel,
        grid=(B, S // tm, nf),
        in_specs=[
            pl.BlockSpec((1, tm, D), lambda b, s, f: (b, s, 0)),
            pl.BlockSpec((1, D), lambda b, s, f: (0, 0)),
            pl.BlockSpec((None, D, tf), lambda b, s, f: (layer, 0, f)),
            pl.BlockSpec((None, D, tf), lambda b, s, f: (layer, 0, nf + f)),
            pl.BlockSpec((None, tf, D), lambda b, s, f: (layer, f, 0)),
        ],
        out_specs=pl.BlockSpec((1, tm, D), lambda b, s, f: (b, s, 0)),
        out_shape=jax.ShapeDtypeStruct((B, S, D), F32),
        scratch_shapes=[pltpu.VMEM((tm, D), BF16)],
        compiler_params=_params(("arbitrary", "arbitrary", "arbitrary")),
        name="ffn",
    )(x, g, w_gu, w_gu, w_down)


def _t5_bucket(dist):
    max_exact = N_BUCKETS // 2
    d = jnp.maximum(dist, 0)
    df = jnp.maximum(d, 1).astype(F32)
    large = max_exact + (jnp.log(df / max_exact) / math.log(MAX_DISTANCE / max_exact)
                         * (N_BUCKETS - max_exact)).astype(jnp.int32)
    large = jnp.minimum(large, N_BUCKETS - 1)
    return jnp.where(d < max_exact, d, large)


def _bias_tables(rel_bias, S):
    by_dist = rel_bias[_t5_bucket(jnp.arange(S))].T.astype(F32) * LOG2E
    nblk, blk, bb = S // MOBA_BLOCK, MOBA_BLOCK, BAND_BLOCK
    rev = jnp.pad(by_dist[:H_B], ((0, 0), (blk, 0)))[:, ::-1]
    tab_b = jnp.stack([rev[:, S - blk * (d + 1): S - blk * (d - 1)] for d in range(nblk)], axis=1)
    tabs_c = []
    for _, r in DILATIONS:
        near = by_dist[H_B:, ::r][:, :bb + 1]
        tabs_c.append(jnp.pad(near[:, ::-1], ((0, 0), (0, bb - 1))))
    return tab_b, jnp.stack(tabs_c, axis=1)


def _proj_colgain(q_gain, k_gain):
    scale = HEAD_DIM ** -0.5
    ones = lambda n: jnp.ones((n * HEAD_DIM,), F32)
    qg, kg = q_gain.astype(F32) * (scale * LOG2E), k_gain.astype(F32)
    return jnp.concatenate([
        ones(H_A) * (scale * LOG2E), ones(H_A), ones(H_A),
        qg[:H_B].reshape(-1), kg[:H_B].reshape(-1), ones(H_B),
        qg[H_B:].reshape(-1), kg[H_B:].reshape(-1), ones(H_C)])[None, :]


def _layer(x, layer, g_mix, w_in, q_gain, k_gain, w_branch, w_out, g_ffn, w_gu, w_down, tab_b, tabs_c):
    qkv, d1, d4, d16, gates = _proj(x, g_mix[None, :], w_in, layer, _proj_colgain(q_gain, k_gain), tm=1024)

    oa = _sb_attention(qkv, blk=256)
    ob = _moba_attention(qkv, tab_b)
    oc = _dilated_attention(d1, d4, d16, tabs_c)

    x = _merge_out(oa, ob, oc, gates, w_branch, w_out, layer, x, tm=256)
    return _ffn(x, g_ffn[None, :], w_gu, w_down, layer, tm=1024, tf=512)


def kernel(x, g_mix, w_in, q_gain, k_gain, w_branch, w_out, g_ffn, w_gu, w_down, rel_bias):
    depth = g_mix.shape[0]
    tab_b, tabs_c = _bias_tables(rel_bias, x.shape[1])
    w_in, w_branch, w_out, w_gu, w_down = (w.astype(BF16) for w in (w_in, w_branch, w_out, w_gu, w_down))
    for l in range(depth):
        x = _layer(x, l, g_mix[l], w_in, q_gain[l], k_gain[l], w_branch, w_out, g_ffn[l], w_gu, w_down, tab_b, tabs_c)
    return x
```

```python
import functools
import math

import jax
import jax.numpy as jnp
from jax import lax
from jax.experimental import pallas as pl
from jax.experimental.pallas import tpu as pltpu

D_MODEL = 2048
HEAD_DIM = 128
H_A = 4
H_B = 6
H_C = 6
N_HEADS = H_A + H_B + H_C
MIX_WIDTH = N_HEADS * HEAD_DIM
N_BRANCH = 3
D_FF = 5632
MOBA_BLOCK = 256
MOBA_TOPK = 3
DILATIONS = ((128, 1), (512, 4), (2048, 16))
BAND_BLOCK = 128
N_BUCKETS = 32
MAX_DISTANCE = 2048
RMS_EPS = 1e-6
NEG_INF = -1e30
LOG2E = math.log2(math.e)

LANES = 128
VMEM_LIMIT = 56 * 1024 * 1024

SB_SKIP_SUM = 110.0

F32 = jnp.float32
BF16 = jnp.bfloat16

SLOTS_MAIN = 3 * (H_A + H_B)
SLOTS_DIL = 3 * H_C
SB_Q, SB_K, SB_V = 0, H_A, 2 * H_A
MB_Q, MB_K, MB_V = 3 * H_A, 3 * H_A + H_B, 3 * H_A + 2 * H_B


def _params(sem):
    return pltpu.CompilerParams(dimension_semantics=sem, vmem_limit_bytes=VMEM_LIMIT)


def _dot(a, b):
    return jnp.dot(a, b, preferred_element_type=F32)


def _dot_nt(a, b):
    return lax.dot_general(a, b, (((1,), (1,)), ((), ())), preferred_element_type=F32)


def _rows(n, size, length=None):
    start = n * size if isinstance(n, int) else pl.multiple_of(n * size, size)
    return pl.ds(start, size if length is None else length)


def _rms_rows(x):
    return x * lax.rsqrt(jnp.mean(x * x, axis=-1, keepdims=True) + RMS_EPS)


PROJ_HEADS = 6
PROJ_TN = PROJ_HEADS * LANES
J_MAIN = SLOTS_MAIN // PROJ_HEADS
J_DIL = SLOTS_DIL // PROJ_HEADS
J_GATE = N_BRANCH * D_MODEL // PROJ_TN
NORMED_TILES = (2, 3, J_MAIN, J_MAIN + 1)
PROJ_SUB = 256
PROJ_PIECE = 2 * LANES
PROJ_PIECES_PER_TILE = PROJ_TN // PROJ_PIECE


def _proj_pieces():
    pairs = N_HEADS // 2
    block = lambda part, head: part * pairs + head // 2
    q, k, v = 0, 1, 2
    tiles = [
        [block(q, 0), block(q, 2), block(k, 0)], [block(k, 2), block(v, 0), block(v, 2)],
        *[[block(part, h) for h in range(H_A, H_A + H_B, 2)] for part in (q, k, v)],
        *[[block(part, h) for h in range(H_A + H_B, N_HEADS, 2)] for part in (q, k, v)],
    ]
    gates0 = 3 * pairs
    tiles += [[gates0 + PROJ_PIECES_PER_TILE * t + p for p in range(PROJ_PIECES_PER_TILE)] for t in range(J_GATE)]
    return [b for tile in tiles for b in tile]


def _proj_kernel(src_ref, x_ref, g_ref, w0_ref, w1_ref, w2_ref, cg_ref, main_ref, d1_ref, d4_ref, d16_ref, gates_ref,
                 h_ref, y_ref, y4_ref):
    del src_ref
    j = pl.program_id(2)
    tm = h_ref.shape[0]
    sub = PROJ_SUB

    is_gate = j >= J_MAIN + J_DIL
    is_dil = (j >= J_MAIN) & jnp.logical_not(is_gate)
    is_norm = functools.reduce(jnp.logical_or, [j == t for t in NORMED_TILES])

    def normed_rows(rows):
        h = (_rms_rows(x_ref[0, rows, :]) * g_ref[...]).astype(BF16)
        h_ref[rows, :] = h
        return h

    def sweep(epilogue, lhs=lambda rows: h_ref[rows, :]):
        for mi in range(tm // sub):
            rows = slice(mi * sub, (mi + 1) * sub)
            h = lhs(rows)
            epilogue(mi, rows, jnp.concatenate([_dot(h, w[...]) for w in (w0_ref, w1_ref, w2_ref)], axis=1))

    def heads(acc, normed):
        for hh in range(PROJ_HEADS):
            sl = slice(hh * LANES, (hh + 1) * LANES)
            yield hh, (_rms_rows(acc[:, sl]) if normed else acc[:, sl]) * cg_ref[:, sl]

    @pl.when(is_gate)
    def _():
        def epilogue(mi, rows, acc):
            gates_ref[0, rows, :] = jax.nn.sigmoid(acc).astype(BF16)
        sweep(epilogue)

    def to_main(normed):
        def epilogue(mi, rows, acc):
            for hh, y in heads(acc, normed):
                main_ref[0, hh, rows, :] = y.astype(BF16)
        return epilogue

    assert 0 not in NORMED_TILES

    @pl.when(j == 0)
    def _():
        sweep(to_main(False), normed_rows)

    for normed in (False, True):
        @pl.when((j > 0) & (j < J_MAIN) & (is_norm == normed))
        def _():
            sweep(to_main(normed))

        @pl.when(is_dil & (is_norm == normed))
        def _():
            def epilogue(mi, rows, acc):
                for hh, y in heads(acc, normed):
                    d1_ref[0, hh, 0, rows, :] = y.astype(BF16)
                    y_ref[mi % 2, hh] = y
                    for c in range(4):
                        y4 = y_ref[mi % 2, hh, pl.ds(c, sub // 4, stride=4), :]
                        d4_ref[0, hh, c, mi * (sub // 4):(mi + 1) * (sub // 4), :] = y4.astype(BF16)
                        y4_ref[mi % 2, hh, c] = y4
                        for k in range(4):
                            d16_ref[0, hh, c + 4 * k, mi * (sub // 16):(mi + 1) * (sub // 16), :] = (
                                y4_ref[mi % 2, hh, c, pl.ds(k, sub // 16, stride=4), :].astype(BF16))
            sweep(epilogue)


def _proj(x, g, w_in, layer, colgain, *, tm):
    B, S, D = x.shape
    tn, hp, npc = PROJ_TN, PROJ_HEADS, PROJ_PIECES_PER_TILE
    jd, jg = J_MAIN, J_MAIN + J_DIL
    piece = lambda p: pl.BlockSpec((None, D, PROJ_PIECE), lambda b, s, j, src: (layer, 0, src[npc * j + p]))
    dil = lambda r: pl.BlockSpec((1, hp, r, tm // r, LANES),
                                 lambda b, s, j, src: (b, jnp.clip(j - jd, 0, J_DIL - 1), 0, s, 0))
    dil_shape = lambda r: jax.ShapeDtypeStruct((B, SLOTS_DIL, r, S // r, LANES), BF16)
    grid_spec = pltpu.PrefetchScalarGridSpec(
        num_scalar_prefetch=1,
        grid=(B, S // tm, J_MAIN + J_DIL + J_GATE),
        in_specs=[
            pl.BlockSpec((1, tm, D), lambda b, s, j, src: (b, s, 0)),
            pl.BlockSpec((1, D), lambda b, s, j, src: (0, 0)),
            piece(0), piece(1), piece(2),
            pl.BlockSpec((1, tn), lambda b, s, j, src: (0, jnp.minimum(j, jg - 1))),
        ],
        out_specs=[
            pl.BlockSpec((1, hp, tm, LANES), lambda b, s, j, src: (b, jnp.minimum(j, jd - 1), s, 0)),
            dil(1), dil(4), dil(16),
            pl.BlockSpec((1, tm, tn), lambda b, s, j, src: (b, s, jnp.maximum(j - jg, 0))),
        ],
        scratch_shapes=[pltpu.VMEM((tm, D), BF16), pltpu.VMEM((2, hp, PROJ_SUB, LANES), F32),
                        pltpu.VMEM((2, hp, 4, PROJ_SUB // 4, LANES), F32)],
    )
    return pl.pallas_call(
        _proj_kernel,
        grid_spec=grid_spec,
        out_shape=[
            jax.ShapeDtypeStruct((B, SLOTS_MAIN, S, LANES), BF16),
            dil_shape(1), dil_shape(4), dil_shape(16),
            jax.ShapeDtypeStruct((B, S, N_BRANCH * D_MODEL), BF16),
        ],
        compiler_params=_params(("arbitrary", "arbitrary", "arbitrary")),
        name="proj",
    )(jnp.asarray(_proj_pieces(), jnp.int32), x, g, w_in, w_in, w_in, colgain)


def _softplus2(z):
    return jnp.maximum(z, 0.0) + jnp.log2(1.0 + jnp.exp2(-jnp.abs(z)))


def _sb_kernel(q_ref, k_ref, v_ref, u_ref, o_ref, *, blk):
    i = pl.program_id(1)
    u = u_ref[...]
    row = lax.broadcasted_iota(jnp.int32, (blk, blk), 0)
    col = lax.broadcasted_iota(jnp.int32, (blk, blk), 1)
    past = col < row

    def blocks(j, state, diag):
        keys = pl.ds(pl.multiple_of(j * blk, blk), blk)
        heads = range(H_A)
        z = [_dot_nt(q_ref[0, h], k_ref[0, h, keys, :]) for h in heads]
        sp = [_softplus2(z[h]) for h in heads]
        spm = [jnp.where(past, sp[h], 0.0) for h in heads] if diag else sp
        hi = [spm[h].astype(BF16) for h in heads]
        lo = [(spm[h] - hi[h].astype(F32)).astype(BF16) for h in heads]
        after = [_dot(hi[h], u) + _dot(lo[h], u) for h in heads]
        w = [jnp.exp2(z[h] - sp[h] - after[h] - state[h][0]) for h in heads]
        if diag:
            w = [jnp.where(past, w[h], 0.0) for h in heads]
        acc = [state[h][1] + _dot(w[h].astype(BF16), v_ref[0, h, keys, :]) for h in heads]
        c = [state[h][0] + after[h][:, :1] + spm[h][:, :1] for h in heads]
        return tuple(zip(c, acc))

    def smallest(state):
        return functools.reduce(jnp.minimum, [jnp.min(c) for c, _ in state])

    zeros = (jnp.zeros((blk, 1), F32), jnp.zeros((blk, HEAD_DIM), F32))
    state = blocks(i, (zeros,) * H_A, True)

    def cond(carry):
        j, c_min, _ = carry
        return (j >= 0) & (c_min < SB_SKIP_SUM * LOG2E)

    def body(carry):
        j, _, state = carry
        state = blocks(j, state, False)
        return j - 1, smallest(state), state

    _, _, state = lax.while_loop(cond, body, (i - 1, smallest(state), state))
    for h, (_, acc) in enumerate(state):
        o_ref[0, :, h * LANES:(h + 1) * LANES] = acc.astype(BF16)


def _sb_attention(qkv, *, blk):
    B, _, S, _ = qkv.shape
    tri = (jnp.arange(blk)[:, None] > jnp.arange(blk)[None, :]).astype(BF16)
    return pl.pallas_call(
        functools.partial(_sb_kernel, blk=blk),
        grid=(B, S // blk),
        in_specs=[
            pl.BlockSpec((1, H_A, blk, LANES), lambda b, i: (b, SB_Q // H_A, i, 0)),
            pl.BlockSpec((1, H_A, S, LANES), lambda b, i: (b, SB_K // H_A, 0, 0)),
            pl.BlockSpec((1, H_A, S, LANES), lambda b, i: (b, SB_V // H_A, 0, 0)),
            pl.BlockSpec((blk, blk), lambda b, i: (0, 0)),
        ],
        out_specs=pl.BlockSpec((1, blk, H_A * LANES), lambda b, i: (b, i, 0)),
        out_shape=jax.ShapeDtypeStruct((B, S, H_A * HEAD_DIM), BF16),
        compiler_params=_params(("arbitrary", "arbitrary")),
        name="sb_attention",
    )(qkv, qkv, qkv, tri)


def _toeplitz(row_vals, n_rows, shift):
    full = jnp.broadcast_to(row_vals, (n_rows, row_vals.shape[1]))
    return pltpu.roll(full, shift, 1, stride=1, stride_axis=0)


MOBA_STRIP = 64
GATE_UNROLL = 8


def _moba_kernel(q_ref, k_ref, v_ref, tab_ref, o_ref, tb_ref, ob_ref, ka_ref, va_ref, qa_ref, m_ref, accl_ref, sc_ref,
                 p_ref, al_ref, *, nblk, n_sel):
    blk = MOBA_BLOCK
    wide = 2 * blk
    npair = nblk // 2

    row = lax.broadcasted_iota(jnp.int32, (blk, blk), 0)
    col = lax.broadcasted_iota(jnp.int32, (blk, blk), 1)

    @pl.when(pl.program_id(1) == 0)
    def _():
        for d in range(nblk):
            tb_ref[d] = _toeplitz(tab_ref[0, d:d + 1, :], blk, blk + 1)[:, :blk]
        diag = jnp.where(col <= row, tb_ref[0], NEG_INF)
        ob_ref[:blk, :blk] = diag
        ob_ref[:blk, blk:] = jnp.zeros((blk, blk), F32)
        ob_ref[blk:, :blk] = tb_ref[1]
        ob_ref[blk:, blk:] = diag

    lane = lax.broadcasted_iota(jnp.int32, (blk, LANES), 1)
    va_ref[:, LANES:] = jnp.ones((va_ref.shape[0], LANES), BF16)
    means = []
    for n in range(nblk):
        rows = slice(n * blk, (n + 1) * blk)
        kb = k_ref[0, 0, rows, :]
        means.append(jnp.mean(kb.astype(F32), axis=0, keepdims=True))
        ka_ref[rows, :LANES] = kb
        ka_ref[rows, LANES:] = jnp.where(lane == n, 1.0, 0.0).astype(BF16)
        va_ref[rows, :LANES] = v_ref[0, 0, rows, :]
    km = jnp.concatenate(means, axis=0).astype(BF16)

    blk_id = lax.broadcasted_iota(jnp.int32, (nblk, blk), 0)
    eye = jnp.where(row == col, 1.0, 0.0).astype(BF16)

    def gating(g, _):
        ids = [g * GATE_UNROLL + k for k in range(GATE_UNROLL)]
        windows = [pl.ds(pl.multiple_of(g * (GATE_UNROLL * blk), GATE_UNROLL * blk) + k * blk, blk)
                   for k in range(GATE_UNROLL)]
        q = [q_ref[0, 0, rows, :] for rows in windows]
        fully_past = [blk_id < i for i in ids]
        gate = [jnp.where(fp, _dot_nt(km, x), NEG_INF) for fp, x in zip(fully_past, q)]
        sel = [jnp.zeros((nblk, blk), F32) for _ in ids]
        for _ in range(n_sel):
            top = [jnp.max(x, axis=0, keepdims=True) for x in gate]
            first = [jnp.min(jnp.where(x == t, blk_id, nblk), axis=0, keepdims=True)
                     for x, t in zip(gate, top)]
            pick = [blk_id == f for f in first]
            sel = [jnp.where(pk, 1.0, x) for pk, x in zip(pick, sel)]
            gate = [jnp.where(pk, -jnp.inf, x) for pk, x in zip(pick, gate)]
        pad = jnp.zeros((LANES - nblk, blk), BF16)
        pen_t = [jnp.concatenate([jnp.where((blk_id == i) | (fp & (x > 0.0)), 0.0, NEG_INF).astype(BF16), pad], axis=0)
                 for i, fp, x in zip(ids, fully_past, sel)]
        pen = [_dot_nt(eye, x).astype(BF16) for x in pen_t]
        for rows, x, y in zip(windows, q, pen):
            qa_ref[rows, :LANES] = x
            qa_ref[rows, LANES:] = y
        return 0

    lax.fori_loop(0, nblk // GATE_UNROLL, gating, 0)


    def aligned(x, m):
        return x if isinstance(x, int) else pl.multiple_of(x, m)

    def pair_bias(d):
        return jnp.concatenate([jnp.concatenate([tb_ref[d], tb_ref[d - 1]], axis=1),
                                jnp.concatenate([tb_ref[d + 1], tb_ref[d]], axis=1)], axis=0)

    def soften(t, j, slot):
        s_base = aligned(j * wide, wide)
        q_base = s_base if t is None else aligned((t + 1 + j) * wide, wide)
        for st in range(wide // MOBA_STRIP):
            off = st * MOBA_STRIP
            s = sc_ref[pl.ds(s_base + off, MOBA_STRIP), :]
            state = pl.ds(q_base + off, MOBA_STRIP)
            m = jnp.broadcast_to(jnp.max(s, axis=-1, keepdims=True), (MOBA_STRIP, LANES))
            if t is not None:
                m_old = m_ref[state, :]
                m = jnp.maximum(m_old, m)
                al_ref[slot, off:off + MOBA_STRIP, :] = jnp.exp2(m_old - m)
            p_ref[slot, off:off + MOBA_STRIP, :] = jnp.exp2(s - jnp.concatenate([m] * (wide // LANES), axis=1)).astype(BF16)
            m_ref[state, :] = m

    def accumulate(t, j, slot):
        if t is None:
            rows = _rows(j, wide)
            accl_ref[rows, :] = _dot(p_ref[slot], va_ref[rows, :])
        else:
            rows = _rows(t + 1 + j, wide)
            al = al_ref[slot]
            accl_ref[rows, :] = (jnp.concatenate([al, al], axis=1) * accl_ref[rows, :]
                                 + _dot(p_ref[slot], va_ref[t * wide:(t + 1) * wide, :]))

    def phase(t, n, score):
        for j in range(n):
            score(j)
        soften(t, 0, 0)

        def body(j, _):
            accumulate(t, j - 1, (j - 1) % 2)
            soften(t, j, j % 2)
            return 0

        lax.fori_loop(1, n, body, 0)
        accumulate(t, n - 1, (n - 1) % 2)

    def own_score(u):
        rows = slice(u * wide, (u + 1) * wide)
        sc_ref[rows, :] = _dot_nt(qa_ref[rows, :], ka_ref[rows, :]) + ob_ref[...]

    phase(None, npair, own_score)
    for t in range(npair - 1):
        def past_score(j):
            rows = slice((t + 1 + j) * wide, (t + 2 + j) * wide)
            sc_ref[j * wide:(j + 1) * wide, :] = (_dot_nt(qa_ref[rows, :], ka_ref[t * wide:(t + 1) * wide, :])
                                                  + pair_bias(2 + 2 * j))

        phase(t, npair - 1 - t, past_score)


    def finish(i, _):
        rows = _rows(i, blk)
        o_ref[0, rows, :] = (accl_ref[rows, :LANES] / accl_ref[rows, LANES:]).astype(BF16)
        return 0

    lax.fori_loop(0, nblk, finish, 0)


def _moba_attention(qkv, tab):
    B, _, S, _ = qkv.shape
    blk = MOBA_BLOCK
    nblk = S // blk
    assert nblk % 2 == 0 and nblk % GATE_UNROLL == 0 and nblk <= LANES
    n_sel = min(MOBA_TOPK, nblk - 1)
    head = lambda slot: pl.BlockSpec((1, 1, S, LANES), lambda h, b: (b, slot + h, 0, 0))
    return pl.pallas_call(
        functools.partial(_moba_kernel, nblk=nblk, n_sel=n_sel),
        grid=(H_B, B),
        in_specs=[head(MB_Q), head(MB_K), head(MB_V), pl.BlockSpec((1, nblk, 2 * blk), lambda h, b: (h, 0, 0))],
        out_specs=pl.BlockSpec((1, S, LANES), lambda h, b: (b, 0, h)),
        out_shape=jax.ShapeDtypeStruct((B, S, H_B * HEAD_DIM), BF16),
        scratch_shapes=[pltpu.VMEM((nblk, blk, blk), F32), pltpu.VMEM((2 * blk, 2 * blk), F32),
                        pltpu.VMEM((S, 2 * LANES), BF16), pltpu.VMEM((S, 2 * LANES), BF16),
                        pltpu.VMEM((S, 2 * LANES), BF16), pltpu.VMEM((S, LANES), F32),
                        pltpu.VMEM((S, 2 * LANES), F32), pltpu.VMEM((S, 2 * blk), F32),
                        pltpu.VMEM((2, 2 * blk, 2 * blk), BF16), pltpu.VMEM((2, 2 * blk, LANES), F32)],
        compiler_params=_params(("arbitrary", "arbitrary")),
        name="moba_attention",
    )(qkv, qkv, qkv, tab)


DIL_UNROLL = 32


def _dilated_kernel(q1_ref, k1_ref, v1_ref, q4_ref, k4_ref, v4_ref, q16_ref, k16_ref, v16_ref, tab_ref, o_ref,
                    c4o_ref, c4l_ref, c16o_ref, c16l_ref, s4o_ref, s4l_ref, s16o_ref, s16l_ref, *, seq):
    bb = BAND_BLOCK
    qi = lax.broadcasted_iota(jnp.int32, (bb, 2 * bb), 0)
    kj = lax.broadcasted_iota(jnp.int32, (bb, 2 * bb), 1)
    visible = (kj >= qi) & (kj <= qi + bb)

    def pattern(g, r, q_ref, k_ref, v_ref, emit):
        bias = jnp.where(visible, _toeplitz(tab_ref[0, g:g + 1, :], bb, 0), NEG_INF)
        bias_first = jnp.concatenate([bias[:, bb:], jnp.full((bb, bb), NEG_INF, F32)], axis=1)
        nb = seq // r // bb

        def body(t, _):
            blocks = [((t * DIL_UNROLL + k) // nb, (t * DIL_UNROLL + k) % nb) for k in range(DIL_UNROLL)]
            keys = [_rows(jnp.maximum(n - 1, 0), bb, 2 * bb) for _, n in blocks]
            s = [_dot_nt(q_ref[0, 0, c, _rows(n, bb), :], k_ref[0, 0, c, kw, :]) + jnp.where(n == 0, bias_first, bias)
                 for (c, n), kw in zip(blocks, keys)]
            m = [jnp.max(x, axis=-1, keepdims=True) for x in s]
            p = [jnp.exp2(x - mx) for x, mx in zip(s, m)]
            den = [jnp.sum(x, axis=-1, keepdims=True) for x in p]
            o = [_dot(x.astype(BF16), v_ref[0, 0, c, kw, :]) for x, (c, _), kw in zip(p, blocks, keys)]
            for (c, n), ox, dx, mx in zip(blocks, o, den, m):
                emit(c, n, ox / dx, mx + jnp.log2(dx))
            return 0

        lax.fori_loop(0, r * nb // DIL_UNROLL, body, 0)

    def class_major(o_cm, l_cm):
        def emit(c, n, o, lse):
            o_cm[c, _rows(n, bb), :] = o
            l_cm[c, _rows(n, bb), :] = jnp.broadcast_to(lse, (bb, LANES))
        return emit

    def to_sequence(cm4, sq):
        for c in range(4):
            sq[pl.ds(c, seq // 4, stride=4), :] = cm4[c]

    def to_classes_of_4(cm16, cm4):
        for c in range(4):
            for k in range(4):
                cm4[c, pl.ds(k, seq // 16, stride=4), :] = cm16[c + 4 * k]

    pattern(2, 16, q16_ref, k16_ref, v16_ref, class_major(c16o_ref, c16l_ref))
    for cm16, sq in ((c16o_ref, s16o_ref), (c16l_ref, s16l_ref)):
        to_classes_of_4(cm16, c4o_ref)
        to_sequence(c4o_ref, sq)
    pattern(1, 4, q4_ref, k4_ref, v4_ref, class_major(c4o_ref, c4l_ref))
    to_sequence(c4o_ref, s4o_ref)
    to_sequence(c4l_ref, s4l_ref)

    def merge(c, n, o1, l1):
        rows = _rows(n, bb)
        l4, l16 = s4l_ref[rows, :], s16l_ref[rows, :]
        top = jnp.maximum(jnp.maximum(l1, l4), l16)
        e1, e4, e16 = jnp.exp2(l1 - top), jnp.exp2(l4 - top), jnp.exp2(l16 - top)
        mix = e1 * o1 + e4 * s4o_ref[rows, :] + e16 * s16o_ref[rows, :]
        o_ref[0, rows, :] = (mix / (e1 + e4 + e16)).astype(BF16)

    pattern(0, 1, q1_ref, k1_ref, v1_ref, merge)


def _dilated_attention(d1, d4, d16, tabs):
    B, S = d1.shape[0], d1.shape[3]
    assert [r for _, r in DILATIONS] == [1, 4, 16] and all(w // r == BAND_BLOCK for w, r in DILATIONS)
    assert (S // BAND_BLOCK) % DIL_UNROLL == 0
    head = lambda r, part: pl.BlockSpec((1, 1, r, S // r, LANES), lambda b, h: (b, part * H_C + h, 0, 0, 0))
    class_major = lambda r: pltpu.VMEM((r, S // r, LANES), F32)
    return pl.pallas_call(
        functools.partial(_dilated_kernel, seq=S),
        grid=(B, H_C),
        in_specs=[head(r, part) for r in (1, 4, 16) for part in range(3)]
        + [pl.BlockSpec((1, len(DILATIONS), 2 * BAND_BLOCK), lambda b, h: (h, 0, 0))],
        out_specs=pl.BlockSpec((1, S, LANES), lambda b, h: (b, 0, h)),
        out_shape=jax.ShapeDtypeStruct((B, S, H_C * HEAD_DIM), BF16),
        scratch_shapes=[class_major(4), class_major(4), class_major(16), class_major(16)]
        + [pltpu.VMEM((S, LANES), F32)] * 4,
        compiler_params=_params(("arbitrary", "arbitrary")),
        name="dilated_attention",
    )(d1, d1, d1, d4, d4, d4, d16, d16, d16, tabs)


def _merge_kernel(oa_ref, ob_ref, oc_ref, g0_ref, g1_ref, g2_ref, wbr_ref, wo_ref, x_ref, out_ref):
    ea, eb = H_A * HEAD_DIM, (H_A + H_B) * HEAD_DIM
    merged = (g0_ref[0].astype(F32) * _dot(oa_ref[0], wbr_ref[:ea, :])
              + g1_ref[0].astype(F32) * _dot(ob_ref[0], wbr_ref[ea:eb, :])
              + g2_ref[0].astype(F32) * _dot(oc_ref[0], wbr_ref[eb:, :]))
    out_ref[0] = x_ref[0] + _dot(merged.astype(BF16), wo_ref[...])


def _merge_out(oa, ob, oc, gates, w_branch, wo, layer, x, *, tm):
    B, S, D = x.shape
    rows = lambda w: pl.BlockSpec((1, tm, w), lambda b, s: (b, s, 0))
    gate = lambda g: pl.BlockSpec((1, tm, D), lambda b, s: (b, s, g))
    resident = lambda w: pl.BlockSpec((None,) + w.shape[1:], lambda b, s: (layer, 0, 0), pipeline_mode=pl.Buffered(1))
    return pl.pallas_call(
        _merge_kernel,
        grid=(B, S // tm),
        in_specs=[rows(oa.shape[-1]), rows(ob.shape[-1]), rows(oc.shape[-1]), gate(0), gate(1), gate(2),
                  resident(w_branch), resident(wo), rows(D)],
        out_specs=rows(D),
        out_shape=jax.ShapeDtypeStruct((B, S, D), F32),
        compiler_params=_params(("arbitrary", "arbitrary")),
        name="merge_out",
    )(oa, ob, oc, gates, gates, gates, w_branch, wo, x)


FFN_SUB = 256


def _ffn_kernel(x_ref, g_ref, wg_ref, wu_ref, wd_ref, o_ref, h_ref):
    first = pl.program_id(2) == 0

    def sweep(start):
        for mi in range(h_ref.shape[0] // FFN_SUB):
            rows = slice(mi * FFN_SUB, (mi + 1) * FFN_SUB)
            h, base = start(rows)
            gate = _dot(h, wg_ref[...])
            up = _dot(h, wu_ref[...])
            act = (gate * jax.nn.sigmoid(gate) * up).astype(BF16)
            o_ref[0, rows, :] = base + _dot(act, wd_ref[...])

    @pl.when(first)
    def _():
        def start(rows):
            x = x_ref[0, rows, :]
            h = (_rms_rows(x) * g_ref[...]).astype(BF16)
            h_ref[rows, :] = h
            return h, x
        sweep(start)

    @pl.when(jnp.logical_not(first))
    def _():
        sweep(lambda rows: (h_ref[rows, :], o_ref[0, rows, :]))


def _ffn(x, g, w_gu, w_down, layer, *, tm, tf):
    B, S, D = x.shape
    nf = D_FF // tf
    return pl.pallas_call(
        _ffn_kernel,
        grid=(B, S // tm, nf),
        in_specs=[
            pl.BlockSpec((1, tm, D), lambda b, s, f: (b, s, 0)),
            pl.BlockSpec((1, D), lambda b, s, f: (0, 0)),
            pl.BlockSpec((None, D, tf), lambda b, s, f: (layer, 0, f)),
            pl.BlockSpec((None, D, tf), lambda b, s, f: (layer, 0, nf + f)),
            pl.BlockSpec((None, tf, D), lambda b, s, f: (layer, f, 0)),
        ],
        out_specs=pl.BlockSpec((1, tm, D), lambda b, s, f: (b, s, 0)),
        out_shape=jax.ShapeDtypeStruct((B, S, D), F32),
        scratch_shapes=[pltpu.VMEM((tm, D), BF16)],
        compiler_params=_params(("arbitrary", "arbitrary", "arbitrary")),
        name="ffn",
    )(x, g, w_gu, w_gu, w_down)


def _t5_bucket(dist):
    max_exact = N_BUCKETS // 2
    d = jnp.maximum(dist, 0)
    df = jnp.maximum(d, 1).astype(F32)
    large = max_exact + (jnp.log(df / max_exact) / math.log(MAX_DISTANCE / max_exact)
                         * (N_BUCKETS - max_exact)).astype(jnp.int32)
    large = jnp.minimum(large, N_BUCKETS - 1)
    return jnp.where(d < max_exact, d, large)


def _bias_tables(rel_bias, S):
    by_dist = rel_bias[_t5_bucket(jnp.arange(S))].T.astype(F32) * LOG2E
    nblk, blk, bb = S // MOBA_BLOCK, MOBA_BLOCK, BAND_BLOCK
    rev = jnp.pad(by_dist[:H_B], ((0, 0), (blk, 0)))[:, ::-1]
    tab_b = jnp.stack([rev[:, S - blk * (d + 1): S - blk * (d - 1)] for d in range(nblk)], axis=1)
    tabs_c = []
    for _, r in DILATIONS:
        near = by_dist[H_B:, ::r][:, :bb + 1]
        tabs_c.append(jnp.pad(near[:, ::-1], ((0, 0), (0, bb - 1))))
    return tab_b, jnp.stack(tabs_c, axis=1)


def _proj_colgain(q_gain, k_gain):
    scale = HEAD_DIM ** -0.5
    ones = lambda n: jnp.ones((n * HEAD_DIM,), F32)
    qg, kg = q_gain.astype(F32) * (scale * LOG2E), k_gain.astype(F32)
    return jnp.concatenate([
        ones(H_A) * (scale * LOG2E), ones(H_A), ones(H_A),
        qg[:H_B].reshape(-1), kg[:H_B].reshape(-1), ones(H_B),
        qg[H_B:].reshape(-1), kg[H_B:].reshape(-1), ones(H_C)])[None, :]


def _layer(x, layer, g_mix, w_in, q_gain, k_gain, w_branch, w_out, g_ffn, w_gu, w_down, tab_b, tabs_c):
    qkv, d1, d4, d16, gates = _proj(x, g_mix[None, :], w_in, layer, _proj_colgain(q_gain, k_gain), tm=1024)

    oa = _sb_attention(qkv, blk=256)
    ob = _moba_attention(qkv, tab_b)
    oc = _dilated_attention(d1, d4, d16, tabs_c)

    x = _merge_out(oa, ob, oc, gates, w_branch, w_out, layer, x, tm=256)
    return _ffn(x, g_ffn[None, :], w_gu, w_down, layer, tm=1024, tf=512)


def kernel(x, g_mix, w_in, q_gain, k_gain, w_branch, w_out, g_ffn, w_gu, w_down, rel_bias):
    depth = g_mix.shape[0]
    tab_b, tabs_c = _bias_tables(rel_bias, x.shape[1])
    w_in, w_branch, w_out, w_gu, w_down = (w.astype(BF16) for w in (w_in, w_branch, w_out, w_gu, w_down))
    for l in range(depth):
        x = _layer(x, l, g_mix[l], w_in, q_gain[l], k_gain[l], w_branch, w_out, g_ffn[l], w_gu, w_down, tab_b, tabs_c)
    return x
```
